```python
import jax, jax.numpy as jnp
from jax import lax
import numpy as np

D_MODEL = 4096
BATCH = 4
SEQ = 2048
DEPTH = 4
DEC_BATCH = 8
DEC_SEQ = 1
PAST_LEN = 8192
PAGE_SIZE = 128

HEAD_DIM = 128
N_MIX_HEADS = D_MODEL // HEAD_DIM
MOBA_HEADS = N_MIX_HEADS // 2
NSA_HEADS = N_MIX_HEADS - MOBA_HEADS
NSA_KV_GROUPS = 2
NSA_GROUP_SIZE = NSA_HEADS // NSA_KV_GROUPS
MOBA_WIDTH = MOBA_HEADS * HEAD_DIM
NSA_WIDTH = NSA_HEADS * HEAD_DIM
NSA_KV_WIDTH = NSA_KV_GROUPS * HEAD_DIM
N_IN = 3 * MOBA_WIDTH + NSA_WIDTH + 6 * NSA_KV_WIDTH + 3 * NSA_HEADS
MOBA_BLOCK = 256
MOBA_TOPK = 3
CMP_LEN = 32
CMP_STRIDE = 16
CMP_HIDDEN = 2 * HEAD_DIM
SEL_BLOCK = 64
SEL_TOPK = 15
WINDOW = 512
POOL_WINDOWS = (2, 4, 8, 16)
POOL_GROUP = D_MODEL // len(POOL_WINDOWS)
POOL_BUF = max(POOL_WINDOWS) - 1
D_FF = ((8 * D_MODEL // 3 + 127) // 128) * 128
N_ATTN_LAYERS = (DEPTH + 1) // 2
N_POOL_LAYERS = DEPTH // 2
ALPHA = (2.0 * DEPTH) ** 0.25
BETA = (8.0 * DEPTH) ** -0.25
LN_EPS = 1e-5
SCALE = HEAD_DIM ** -0.5
MOBA_Q_CHUNK = 16
NSA_Q_CHUNK = 64
WIN_Q_BLOCK = 128
NEG = -1e30

kernel_name = 'hybrid_moba_nsa_pool_decoder_step'


def _layernorm(x, g, b):
    xf = x.astype(jnp.float32)
    mu = jnp.mean(xf, axis=-1, keepdims=True)
    var = jnp.mean(jnp.square(xf - mu), axis=-1, keepdims=True)
    return ((xf - mu) * lax.rsqrt(var + LN_EPS) * g + b).astype(x.dtype)


def _swiglu(x, w_gate, w_up, w_down):
    return (jax.nn.silu(x @ w_gate) * (x @ w_up)) @ w_down


def _alibi_slopes(n):
    return jnp.asarray(np.exp2(-8.0 * np.arange(1, n + 1) / n), dtype=jnp.float32)


def _gather_blocks(blocks, idx):
    return jax.vmap(jax.vmap(lambda bl, ix: bl[ix]))(blocks, idx)


def _gather_pages(cache, page_table):
    g = cache[page_table]
    return g.reshape((page_table.shape[0], -1) + cache.shape[2:])


def _map_chunks(fn, q_pos, q, chunk):
    B, T = q.shape[:2]
    c = chunk if T % chunk == 0 else T
    n = T // c
    qc = jnp.moveaxis(q.reshape((B, n, c) + q.shape[2:]), 1, 0)
    out = lax.map(lambda a: fn(a[0], a[1]), (q_pos.reshape(n, c), qc))
    return jax.tree_util.tree_map(lambda o: jnp.moveaxis(o, 0, 1).reshape((B, T) + o.shape[3:]), out)


def _to_blocks(a, blk):
    B, L, X, dh = a.shape
    nb = -(-L // blk)
    a = jnp.pad(a, ((0, 0), (0, nb * blk - L), (0, 0), (0, 0)))
    return a.reshape(B, nb, blk, X, dh).transpose(0, 3, 1, 2, 4)


def _moba_attention(q, q_pos, k, v, slopes):
    B, L, H, dh = k.shape
    kb, vb = _to_blocks(k, MOBA_BLOCK), _to_blocks(v, MOBA_BLOCK)
    nb = kb.shape[2]
    kmean = jnp.mean(kb.astype(jnp.float32), axis=3)
    n_top = min(MOBA_TOPK, nb)

    def chunk(pos, qc):
        Tc = pos.shape[0]
        cur = pos // MOBA_BLOCK
        gate = jnp.einsum('bthd,bhnd->bhtn', qc.astype(jnp.float32), kmean)
        gate = jnp.where(jnp.arange(nb)[None, :] < cur[:, None], gate, NEG)
        _, top = lax.top_k(gate, n_top)
        own = jnp.broadcast_to(cur[None, None, :, None], (B, H, Tc, 1)).astype(top.dtype)
        idx = jnp.concatenate([top, own], axis=-1)
        blk_ok = jnp.concatenate([jnp.arange(n_top)[None, :] < cur[:, None],
                                  jnp.ones((Tc, 1), bool)], axis=-1)
        kg, vg = _gather_blocks(kb, idx), _gather_blocks(vb, idx)
        dist = pos[None, None, :, None, None] - (idx[..., None] * MOBA_BLOCK + jnp.arange(MOBA_BLOCK))
        s = (jnp.einsum('bthd,bhtjkd->bhtjk', qc, kg).astype(jnp.float32) * SCALE
             - slopes[None, :, None, None, None] * dist)
        ok = blk_ok[None, None, :, :, None] & (dist >= 0)
        p = jax.nn.softmax(jnp.where(ok, s, NEG).reshape(B, H, Tc, -1), axis=-1)
        p = p.reshape(B, H, Tc, n_top + 1, MOBA_BLOCK)
        return jnp.einsum('bhtjk,bhtjkd->bthd', p, vg.astype(jnp.float32)).astype(qc.dtype)

    return _map_chunks(chunk, q_pos, q, MOBA_Q_CHUNK)


def _compress(k, pe, w1, w2):
    B, L, G, dh = k.shape
    nc = (L - CMP_LEN) // CMP_STRIDE + 1
    idx = np.arange(nc)[:, None] * CMP_STRIDE + np.arange(CMP_LEN)[None, :]
    blk = k[:, idx] + pe[None, None, :, None, :]
    flat = jnp.moveaxis(blk, 3, 2).reshape(B, nc, G, CMP_LEN * dh)
    return jax.nn.silu(flat @ w1) @ w2


def _nsa_cmp_sel(q, q_pos, kcmp, vcmp, ksel, vsel, slopes):
    B, T, NH, dh = q.shape
    L = ksel.shape[1]
    nc = kcmp.shape[1]
    ns = -(-L // SEL_BLOCK)
    c_start = np.arange(nc) * CMP_STRIDE
    s_start = np.arange(ns) * SEL_BLOCK
    overlap = ((c_start[:, None] <= s_start[None, :] + SEL_BLOCK - 1)
               & (c_start[:, None] + CMP_LEN - 1 >= s_start[None, :]))
    overlap = jnp.asarray(overlap.astype(np.float32))
    c_end = jnp.asarray(c_start + CMP_LEN - 1, dtype=jnp.int32)
    ksb, vsb = _to_blocks(ksel, SEL_BLOCK), _to_blocks(vsel, SEL_BLOCK)
    sl = slopes.reshape(NSA_KV_GROUPS, NSA_GROUP_SIZE)
    n_top = min(SEL_TOPK, ns)

    def chunk(pos, qc):
        Tc = pos.shape[0]
        qg = qc.reshape(B, Tc, NSA_KV_GROUPS, NSA_GROUP_SIZE, dh)
        dist_c = pos[:, None] - c_end[None, :]
        s = (jnp.einsum('btgsd,bngd->bgstn', qg, kcmp).astype(jnp.float32) * SCALE
             - sl[None, :, :, None, None] * dist_c)
        cmask = dist_c >= 0
        p = jnp.where(cmask, jax.nn.softmax(jnp.where(cmask, s, NEG), axis=-1), 0.0)
        o_cmp = jnp.einsum('bgstn,bngd->btgsd', p, vcmp.astype(jnp.float32))
        imp = jnp.einsum('bgstn,nj->bgtj', p, overlap)
        cur = pos // SEL_BLOCK
        imp = jnp.where(jnp.arange(ns)[None, :] < cur[:, None], imp, NEG)
        _, top = lax.top_k(imp, n_top)
        own = jnp.broadcast_to(cur[None, None, :, None], (B, NSA_KV_GROUPS, Tc, 1)).astype(top.dtype)
        idx = jnp.concatenate([top, own], axis=-1)
        blk_ok = jnp.concatenate([jnp.arange(n_top)[None, :] < cur[:, None],
                                  jnp.ones((Tc, 1), bool)], axis=-1)
        kg, vg = _gather_blocks(ksb, idx), _gather_blocks(vsb, idx)
        dist = pos[None, None, :, None, None] - (idx[..., None] * SEL_BLOCK + jnp.arange(SEL_BLOCK))
        s2 = (jnp.einsum('btgsd,bgtjkd->bgstjk', qg, kg).astype(jnp.float32) * SCALE
              - sl[None, :, :, None, None, None] * dist[:, :, None])
        ok = (blk_ok[None, None, :, :, None] & (dist >= 0))[:, :, None]
        p2 = jax.nn.softmax(jnp.where(ok, s2, NEG).reshape(B, NSA_KV_GROUPS, NSA_GROUP_SIZE, Tc, -1), axis=-1)
        p2 = p2.reshape(B, NSA_KV_GROUPS, NSA_GROUP_SIZE, Tc, n_top + 1, SEL_BLOCK)
        o_sel = jnp.einsum('bgstjk,bgtjkd->btgsd', p2, vg.astype(jnp.float32))
        return o_cmp.reshape(B, Tc, NH, dh), o_sel.reshape(B, Tc, NH, dh)

    return _map_chunks(chunk, q_pos, q, NSA_Q_CHUNK)


def _window_core(q, q_pos, k, v, k_pos, slopes):
    B, Tq, NH, dh = q.shape
    qg = q.reshape(B, Tq, NSA_KV_GROUPS, NSA_GROUP_SIZE, dh)
    dist = q_pos[:, None] - k_pos[None, :]
    ok = (dist >= 0) & (dist < WINDOW) & (k_pos >= 0)[None, :]
    s = (jnp.einsum('btgsd,bkgd->bgstk', qg, k).astype(jnp.float32) * SCALE
         - slopes.reshape(NSA_KV_GROUPS, NSA_GROUP_SIZE)[None, :, :, None, None] * dist)
    p = jax.nn.softmax(jnp.where(ok, s, NEG), axis=-1)
    return jnp.einsum('bgstk,bkgd->btgsd', p, v.astype(jnp.float32)).reshape(B, Tq, NH, dh)


def _window_prompt(q, k, v, slopes):
    B, T = q.shape[:2]
    c = WIN_Q_BLOCK if T % WIN_Q_BLOCK == 0 else T
    n = T // c
    pad = ((0, 0), (WINDOW, 0), (0, 0), (0, 0))
    kp, vp = jnp.pad(k, pad), jnp.pad(v, pad)
    pos = jnp.concatenate([jnp.full((WINDOW,), -1, jnp.int32), jnp.arange(T, dtype=jnp.int32)])
    qb = jnp.moveaxis(q.reshape((B, n, c) + q.shape[2:]), 1, 0)

    def block(args):
        qi, i = args
        s0 = i * c
        kb = lax.dynamic_slice_in_dim(kp, s0, WINDOW + c, axis=1)
        vb = lax.dynamic_slice_in_dim(vp, s0, WINDOW + c, axis=1)
        pb = lax.dynamic_slice_in_dim(pos, s0, WINDOW + c)
        return _window_core(qi, s0 + jnp.arange(c, dtype=jnp.int32), kb, vb, pb, slopes)

    out = lax.map(block, (qb, jnp.arange(n, dtype=jnp.int32)))
    return jnp.moveaxis(out, 0, 1).reshape(q.shape)


def _sparse_attn_mixer(h, q_pos, past, w_in, w_out, cmp_pe, cmp_w1, cmp_w2):
    B, T, _ = h.shape
    cuts = [int(c) for c in np.cumsum([MOBA_WIDTH] * 3 + [NSA_WIDTH] + [NSA_KV_WIDTH] * 6)]
    parts = jnp.split(h @ w_in, cuts, axis=-1)
    q_a, k_a, v_a = [p_.reshape(B, T, MOBA_HEADS, HEAD_DIM) for p_ in parts[:3]]
    q_b = parts[3].reshape(B, T, NSA_HEADS, HEAD_DIM)
    k_c, v_c, k_s, v_s, k_w, v_w = [p_.reshape(B, T, NSA_KV_GROUPS, HEAD_DIM) for p_ in parts[4:10]]
    gates = jax.nn.sigmoid(parts[10].astype(jnp.float32)).reshape(B, T, 3, NSA_HEADS, 1)
    moba_new = jnp.stack([k_a, v_a], axis=2)
    nsa_new = jnp.stack([k_c, v_c, k_s, v_s], axis=2)
    win_new = jnp.stack([k_w, v_w], axis=2)
    if past is None:
        moba_all, nsa_all = moba_new, nsa_new
    else:
        moba_past, nsa_past, win_past = past
        moba_all = jnp.concatenate([moba_past, moba_new], axis=1)
        nsa_all = jnp.concatenate([nsa_past, nsa_new], axis=1)
    o_a = _moba_attention(q_a, q_pos, moba_all[:, :, 0], moba_all[:, :, 1], _alibi_slopes(MOBA_HEADS))
    sl_b = _alibi_slopes(NSA_HEADS)
    k_cmp = _compress(nsa_all[:, :, 0], cmp_pe[0], cmp_w1[0], cmp_w2[0])
    v_cmp = _compress(nsa_all[:, :, 1], cmp_pe[1], cmp_w1[1], cmp_w2[1])
    o_cmp, o_sel = _nsa_cmp_sel(q_b, q_pos, k_cmp, v_cmp, nsa_all[:, :, 2], nsa_all[:, :, 3], sl_b)
    if past is None:
        o_win = _window_prompt(q_b, k_w, v_w, sl_b)
        win_state = win_new[:, -min(WINDOW, T):]
    else:
        win_all = jnp.concatenate([win_past, win_new], axis=1)
        n_buf = win_past.shape[1]
        k_pos = q_pos[0] - n_buf + jnp.arange(n_buf + T, dtype=jnp.int32)
        o_win = _window_core(q_b, q_pos, win_all[:, :, 0], win_all[:, :, 1], k_pos, sl_b)
        win_state = win_all[:, -n_buf:]
    o_b = gates[:, :, 0] * o_cmp + gates[:, :, 1] * o_sel + gates[:, :, 2] * o_win
    mixed = jnp.concatenate([o_a.reshape(B, T, MOBA_WIDTH), o_b.reshape(B, T, NSA_WIDTH)], axis=-1)
    return mixed.astype(h.dtype) @ w_out, moba_new, nsa_new, win_state


def _pool_mixer(h, buf, q_pos, w_pool, scale):
    B, T, D = h.shape
    xc = jnp.concatenate([buf, h], axis=1)
    cs = jnp.cumsum(jnp.pad(xc.astype(jnp.float32), ((0, 0), (1, 0), (0, 0))), axis=1)
    P = POOL_BUF
    hf = h.astype(jnp.float32)
    groups = []
    for g, w in enumerate(POOL_WINDOWS):
        c0, c1 = g * POOL_GROUP, (g + 1) * POOL_GROUP
        wsum = cs[:, P + 1:P + 1 + T, c0:c1] - cs[:, P + 1 - w:P + 1 - w + T, c0:c1]
        cnt = jnp.minimum(w, q_pos + 1).astype(jnp.float32)[None, :, None]
        groups.append(wsum / cnt - hf[:, :, c0:c1])
    pooled = jnp.stack(groups, axis=2)
    y = jnp.einsum('btgc,gcd->btgd', pooled, w_pool.astype(jnp.float32)).reshape(B, T, D) * scale
    return y.astype(h.dtype), xc[:, -POOL_BUF:]


def _trunk(x, start, past, ln_g, ln_b, ffn_gate, ffn_up, ffn_down, attn_w_in, attn_w_out,
           cmp_pe, cmp_w1, cmp_w2, pool_w, pool_scale):
    B, T, _ = x.shape
    q_pos = start + jnp.arange(T, dtype=jnp.int32)
    new_moba, new_nsa, new_win, new_pool = [], [], [], []
    for layer in range(DEPTH):
        x = _layernorm(ALPHA * x + 0.5 * _swiglu(x, ffn_gate[layer, 0], ffn_up[layer, 0], ffn_down[layer, 0]),
                       ln_g[layer, 0], ln_b[layer, 0])
        if layer % 2 == 0:
            a = layer // 2
            if past is None:
                layer_past = None
            else:
                cache_moba_kv, cache_nsa_kv, state_nsa_win, _, page_table = past
                layer_past = (_gather_pages(cache_moba_kv[a], page_table),
                              _gather_pages(cache_nsa_kv[a], page_table),
                              state_nsa_win[a])
            y, mkv, nkv, win = _sparse_attn_mixer(x, q_pos, layer_past, attn_w_in[a], attn_w_out[a],
                                                  cmp_pe[a], cmp_w1[a], cmp_w2[a])
            new_moba.append(mkv)
            new_nsa.append(nkv)
            new_win.append(win)
        else:
            p = layer // 2
            buf = jnp.zeros((B, POOL_BUF, D_MODEL), x.dtype) if past is None else past[3][p]
            y, pbuf = _pool_mixer(x, buf, q_pos, pool_w[p], pool_scale[p])
            new_pool.append(pbuf)
        x = _layernorm(ALPHA * x + y, ln_g[layer, 1], ln_b[layer, 1])
        x = _layernorm(ALPHA * x + 0.5 * _swiglu(x, ffn_gate[layer, 1], ffn_up[layer, 1], ffn_down[layer, 1]),
                       ln_g[layer, 2], ln_b[layer, 2])
    return x, jnp.stack(new_moba), jnp.stack(new_nsa), jnp.stack(new_win), jnp.stack(new_pool)


def setup_inputs(seed: int = 0) -> dict:
    key = jax.random.key(seed)
    ks = jax.random.split(key, 20)
    nrm = jax.random.normal
    f32 = jnp.float32
    n_pages = PAST_LEN // PAGE_SIZE
    pool_pages = (DEC_BATCH * n_pages * 5) // 4
    wbuf = min(WINDOW, PAST_LEN)
    col_scale = np.ones((N_IN,), np.float32)
    col_scale[2 * MOBA_WIDTH:3 * MOBA_WIDTH] = BETA
    base = 3 * MOBA_WIDTH + NSA_WIDTH
    for j in (1, 3, 5):
        col_scale[base + j * NSA_KV_WIDTH:base + (j + 1) * NSA_KV_WIDTH] = BETA
    perm = jax.random.permutation(ks[6], pool_pages)
    page_table = perm[:DEC_BATCH * n_pages].reshape(DEC_BATCH, n_pages).astype(jnp.int32)
    n_out = MOBA_WIDTH + NSA_WIDTH
    return {
        'x_prompt': nrm(ks[0], (BATCH, SEQ, D_MODEL), f32),
        'x_sample': nrm(ks[1], (DEC_BATCH, DEC_SEQ, D_MODEL), f32),
        'cache_moba_kv': nrm(ks[2], (N_ATTN_LAYERS, pool_pages, PAGE_SIZE, 2, MOBA_HEADS, HEAD_DIM), f32),
        'cache_nsa_kv': nrm(ks[3], (N_ATTN_LAYERS, pool_pages, PAGE_SIZE, 4, NSA_KV_GROUPS, HEAD_DIM), f32),
        'state_nsa_win': nrm(ks[4], (N_ATTN_LAYERS, DEC_BATCH, wbuf, 2, NSA_KV_GROUPS, HEAD_DIM), f32),
        'state_pool': nrm(ks[5], (N_POOL_LAYERS, DEC_BATCH, POOL_BUF, D_MODEL), f32),
        'page_table': page_table,
        'ln_g': 1.0 + 0.05 * nrm(ks[7], (DEPTH, 3, D_MODEL), f32),
        'ln_b': 0.02 * nrm(ks[8], (DEPTH, 3, D_MODEL), f32),
        'ffn_gate': nrm(ks[9], (DEPTH, 2, D_MODEL, D_FF), f32) * (D_MODEL ** -0.5),
        'ffn_up': nrm(ks[10], (DEPTH, 2, D_MODEL, D_FF), f32) * (BETA * D_MODEL ** -0.5),
        'ffn_down': nrm(ks[11], (DEPTH, 2, D_FF, D_MODEL), f32) * (BETA * D_FF ** -0.5),
        'attn_w_in': nrm(ks[12], (N_ATTN_LAYERS, D_MODEL, N_IN), f32) * (D_MODEL ** -0.5) * jnp.asarray(col_scale),
        'attn_w_out': nrm(ks[13], (N_ATTN_LAYERS, n_out, D_MODEL), f32) * (BETA * n_out ** -0.5),
        'cmp_pe': 0.1 * nrm(ks[14], (N_ATTN_LAYERS, 2, CMP_LEN, HEAD_DIM), f32),
        'cmp_w1': nrm(ks[15], (N_ATTN_LAYERS, 2, CMP_LEN * HEAD_DIM, CMP_HIDDEN), f32) * ((CMP_LEN * HEAD_DIM) ** -0.5),
        'cmp_w2': nrm(ks[16], (N_ATTN_LAYERS, 2, CMP_HIDDEN, HEAD_DIM), f32) * (CMP_HIDDEN ** -0.5),
        'pool_w': nrm(ks[17], (N_POOL_LAYERS, len(POOL_WINDOWS), POOL_GROUP, POOL_GROUP), f32) * (BETA * POOL_GROUP ** -0.5),
        'pool_scale': 1.0 + 0.1 * nrm(ks[18], (N_POOL_LAYERS, D_MODEL), f32),
    }


def reference(x_prompt, x_sample, cache_moba_kv, cache_nsa_kv, state_nsa_win, state_pool, page_table,
              ln_g, ln_b, ffn_gate, ffn_up, ffn_down, attn_w_in, attn_w_out, cmp_pe, cmp_w1, cmp_w2,
              pool_w, pool_scale):
    past_len = page_table.shape[1] * cache_moba_kv.shape[2]
    y_prompt, moba_p, nsa_p, win_p, pool_p = _trunk(
        x_prompt, 0, None, ln_g, ln_b, ffn_gate, ffn_up, ffn_down, attn_w_in, attn_w_out,
        cmp_pe, cmp_w1, cmp_w2, pool_w, pool_scale)
    y_sample, moba_s, nsa_s, win_s, pool_s = _trunk(
        x_sample, past_len, (cache_moba_kv, cache_nsa_kv, state_nsa_win, state_pool, page_table),
        ln_g, ln_b, ffn_gate, ffn_up, ffn_down, attn_w_in, attn_w_out,
        cmp_pe, cmp_w1, cmp_w2, pool_w, pool_scale)
    return (y_prompt, y_sample, moba_p, nsa_p, win_p, pool_p, moba_s, nsa_s, win_s, pool_s)
```

```python
import functools

import jax
import jax.numpy as jnp
import numpy as np
from jax import lax
from jax.experimental import pallas as pl
from jax.experimental.pallas import tpu as pltpu

F32 = jnp.float32
BF16 = jnp.bfloat16
MXU_DTYPE = BF16

HEAD_DIM = 128
MOBA_BLOCK = 256
MOBA_TOPK = 3
CMP_LEN = 32
CMP_STRIDE = 16
SEL_BLOCK = 64
SEL_TOPK = 15
WINDOW = 512
POOL_WINDOWS = (2, 4, 8, 16)
POOL_BUF = max(POOL_WINDOWS) - 1
LN_EPS = 1e-5
SCALE = HEAD_DIM ** -0.5
NEG = -1e30
NSA_Q_TILE = 128
NSA_KEY_CHUNK = 512
LN_COL_CHUNK = 1024
LN_ROW_CHUNK = 128

V7X_VMEM_BYTES = 64 * 1024 * 1024
VMEM_LIMIT = V7X_VMEM_BYTES - 8 * 1024 * 1024


def _params(*sem):
    return pltpu.CompilerParams(dimension_semantics=sem, vmem_limit_bytes=VMEM_LIMIT)


def _mx(a):
    return a.astype(MXU_DTYPE)


def _dot(a, b):
    return jnp.dot(_mx(a), _mx(b), preferred_element_type=F32)


def _dot_nt(a, b):
    return lax.dot_general(_mx(a), _mx(b), (((1,), (1,)), ((), ())), preferred_element_type=F32)


def _alibi_slopes(n):
    return jnp.asarray(np.exp2(-8.0 * np.arange(1, n + 1) / n), dtype=F32)


def _layernorm_rows(z, g, b):
    mu = jnp.mean(z, axis=-1, keepdims=True)
    zc = z - mu
    var = jnp.mean(zc * zc, axis=-1, keepdims=True)
    return zc * lax.rsqrt(var + LN_EPS) * g + b


def _ffn_up_kernel(x_ref, wg_ref, wu_ref, o_ref):
    x = x_ref[...]
    g = _dot(x, wg_ref[...])
    u = _dot(x, wu_ref[...])
    o_ref[...] = (g * jax.nn.sigmoid(g) * u).astype(o_ref.dtype)


def _ffn_up(xb, w_gate, w_up, layer, sub, tm, tn=256):
    M, D = xb.shape
    F = w_gate.shape[-1]
    assert M % tm == 0 and F % tn == 0
    wspec = pl.BlockSpec((None, None, D, tn), lambda i, j: (layer, sub, 0, j))
    return pl.pallas_call(
        _ffn_up_kernel,
        grid=(M // tm, F // tn),
        in_specs=[pl.BlockSpec((tm, D), lambda i, j: (i, 0)), wspec, wspec],
        out_specs=pl.BlockSpec((tm, tn), lambda i, j: (i, j)),
        out_shape=jax.ShapeDtypeStruct((M, F), BF16),
        compiler_params=_params("parallel", "arbitrary"),
        name="ffn_up",
    )(xb, w_gate, w_up)


def _matmul_res_ln_kernel(h_ref, w_ref, res_ref, g_ref, b_ref, o_ref, ob_ref, *, alpha, coef, nk):
    k = pl.program_id(1)
    h = h_ref[...]
    d = o_ref.shape[1]
    cw = min(d, LN_COL_CHUNK)
    for c0 in range(0, d, cw):
        part = _dot(h, w_ref[:, c0:c0 + cw])

        @pl.when(k == 0)
        def _():
            o_ref[:, c0:c0 + cw] = part

        @pl.when(k > 0)
        def _():
            o_ref[:, c0:c0 + cw] += part

    @pl.when(k == nk - 1)
    def _():
        tm = o_ref.shape[0]
        rc = min(tm, LN_ROW_CHUNK)
        for r0 in range(0, tm, rc):
            z = alpha * res_ref[r0:r0 + rc, :] + coef * o_ref[r0:r0 + rc, :]
            y = _layernorm_rows(z, g_ref[...], b_ref[...])
            o_ref[r0:r0 + rc, :] = y
            ob_ref[r0:r0 + rc, :] = y.astype(ob_ref.dtype)


def _matmul_res_ln(h, w, widx, res, g, b, *, alpha, coef, tm, tk):
    M, K = h.shape
    D = w.shape[-1]
    assert M % tm == 0 and K % tk == 0
    nk = K // tk
    nlead = len(widx)
    wspec = pl.BlockSpec((None,) * nlead + (tk, D), lambda i, k: tuple(widx) + (k, 0))
    row = pl.BlockSpec((1, D), lambda i, k: (0, 0))
    full = pl.BlockSpec((tm, D), lambda i, k: (i, 0))
    res_spec = pl.BlockSpec((tm, D), lambda i, k: (i, 0), pipeline_mode=pl.Buffered(1))
    return pl.pallas_call(
        functools.partial(_matmul_res_ln_kernel, alpha=alpha, coef=coef, nk=nk),
        grid=(M // tm, nk),
        in_specs=[pl.BlockSpec((tm, tk), lambda i, k: (i, k)), wspec, res_spec, row, row],
        out_specs=[full, full],
        out_shape=[jax.ShapeDtypeStruct((M, D), F32), jax.ShapeDtypeStruct((M, D), BF16)],
        compiler_params=_params("parallel", "arbitrary"),
        name="matmul_res_ln",
    )(h, w, res, g.reshape(1, D), b.reshape(1, D))


def _proj_kernel(x_ref, w_ref, o_ref):
    y = _dot(x_ref[...], w_ref[...])
    o_ref[...] = y[:, :o_ref.shape[1]].astype(o_ref.dtype)


def _proj(xb, w, layer, col0, ncols, out_dtype, tm, tn):
    M, D = xb.shape
    assert M % tm == 0 and col0 % tn == 0
    if ncols % tn == 0:
        nj, out_tn = ncols // tn, tn
    else:
        assert ncols < tn
        nj, out_tn = 1, ncols
    j0 = col0 // tn
    return pl.pallas_call(
        _proj_kernel,
        grid=(M // tm, nj),
        in_specs=[pl.BlockSpec((tm, D), lambda i, j: (i, 0)),
                  pl.BlockSpec((None, D, tn), lambda i, j: (layer, 0, j0 + j))],
        out_specs=pl.BlockSpec((tm, out_tn), lambda i, j: (i, j)),
        out_shape=jax.ShapeDtypeStruct((M, ncols), out_dtype),
        compiler_params=_params("parallel", "arbitrary"),
        name="in_proj",
    )(xb, w)


def _pool_kernel(x_ref, prev_ref, w_ref, sc_ref, g_ref, b_ref, o_ref, ob_ref, *,
                 alpha, start, tt, decode, group):
    ti = pl.program_id(1)
    zs = []
    for gi, win in enumerate(POOL_WINDOWS):
        c0, c1 = gi * group, (gi + 1) * group
        prev = prev_ref[0, :, c0:c1]
        if decode:
            xg = prev[POOL_BUF:POOL_BUF + 1]
            wsum = jnp.sum(prev[POOL_BUF + 1 - win:], axis=0, keepdims=True)
            cnt = jnp.full((1, 1), float(min(win, start + 1)), F32)
        else:
            xg = x_ref[0, :, c0:c1]
            prev = jnp.where(ti == 0, 0.0, prev)
            s = jnp.concatenate([prev, xg], axis=0)
            span = 1
            while span < win:
                s = s + pltpu.roll(s, span, axis=0)
                span *= 2
            wsum = s[POOL_BUF + 1:]
            pos = start + ti * tt + lax.broadcasted_iota(jnp.int32, (tt, 1), 0)
            cnt = jnp.minimum(win, pos + 1).astype(F32)
        pooled = wsum / cnt - xg
        y = _dot(pooled, w_ref[gi]) * sc_ref[:, c0:c1]
        zs.append(alpha * xg + y)
    d = group * len(POOL_WINDOWS)
    mu = sum(jnp.sum(z, axis=-1, keepdims=True) for z in zs) / d
    var = sum(jnp.sum((z - mu) * (z - mu), axis=-1, keepdims=True) for z in zs) / d
    inv = lax.rsqrt(var + LN_EPS)
    for gi, z in enumerate(zs):
        c0, c1 = gi * group, (gi + 1) * group
        y = (z - mu) * inv * g_ref[:, c0:c1] + b_ref[:, c0:c1]
        o_ref[0, :, c0:c1] = y
        ob_ref[0, :, c0:c1] = y.astype(ob_ref.dtype)


def _pool_mixer_ln(x3, prev3, w_pool, scale, g, b, *, alpha, start, tt, decode):
    B, T, D = x3.shape
    group = D // len(POOL_WINDOWS)
    assert T % tt == 0 and (decode or tt % 16 == 0)
    per = tt // 16 if not decode else 0
    if decode:
        prev_map = lambda bi, ti: (bi, 0, 0)
    else:
        prev_map = lambda bi, ti: (bi, jnp.maximum(ti * per - 1, 0), 0)
    row = pl.BlockSpec((1, D), lambda bi, ti: (0, 0))
    tile = pl.BlockSpec((1, tt, D), lambda bi, ti: (bi, ti, 0))
    return pl.pallas_call(
        functools.partial(_pool_kernel, alpha=alpha, start=start, tt=tt, decode=decode, group=group),
        grid=(B, T // tt),
        in_specs=[tile, pl.BlockSpec((1, 16, D), prev_map),
                  pl.BlockSpec(w_pool.shape, lambda bi, ti: (0, 0, 0)), row, row, row],
        out_specs=[tile, tile],
        out_shape=[jax.ShapeDtypeStruct((B, T, D), F32), jax.ShapeDtypeStruct((B, T, D), BF16)],
        compiler_params=_params("parallel", "arbitrary"),
        name="pool_mixer_ln",
    )(x3, prev3, w_pool, scale.reshape(1, D), g.reshape(1, D), b.reshape(1, D))


def _topk_mask(score, col, n_valid, k):
    n = score.shape[1]
    masked = jnp.where(col < n_valid, score, NEG)
    rank = jnp.zeros(score.shape, jnp.int32)
    for j in range(n):
        other = masked[:, j:j + 1]
        ahead = (other > masked) | ((other == masked) & (j < col))
        rank = rank + ahead.astype(jnp.int32)
    return (rank < k) & (col < n_valid)


def _online_update(m_ref, l_ref, acc_ref, s, ok, v):
    sm = jnp.where(ok, s, NEG)
    m_old = m_ref[...]
    m_new = jnp.maximum(m_old, jnp.max(sm, axis=-1, keepdims=True))
    p = jnp.where(ok, jnp.exp(sm - m_new), 0.0)
    a = jnp.exp(m_old - m_new)
    l_ref[...] = a * l_ref[...] + jnp.sum(p, axis=-1, keepdims=True)
    acc_ref[...] = a * acc_ref[...] + _dot(p, v)
    m_ref[...] = m_new


def _moba_kernel(slopes_ref, q_ref, k_ref, v_ref, o_ref, m_ref, l_ref, acc_ref, sel_ref, *, nb):
    h = pl.program_id(1)
    qi = pl.program_id(2)
    slope = slopes_ref[h]
    q = q_ref[0]
    kmean = jnp.concatenate(
        [jnp.sum(k_ref[0, n * MOBA_BLOCK:(n + 1) * MOBA_BLOCK, :], axis=0, keepdims=True) for n in range(nb)],
        axis=0) * (1.0 / MOBA_BLOCK)
    gate = _dot_nt(q, kmean)
    col = lax.broadcasted_iota(jnp.int32, gate.shape, 1)
    sel_ref[...] = _topk_mask(gate, col, qi, MOBA_TOPK).astype(F32)

    m_ref[...] = jnp.full(m_ref.shape, NEG, F32)
    l_ref[...] = jnp.zeros(l_ref.shape, F32)
    acc_ref[...] = jnp.zeros(acc_ref.shape, F32)
    rq = lax.broadcasted_iota(jnp.int32, (MOBA_BLOCK, MOBA_BLOCK), 0)
    ck = lax.broadcasted_iota(jnp.int32, (MOBA_BLOCK, MOBA_BLOCK), 1)

    def block(n, carry):
        start = pl.multiple_of(n * MOBA_BLOCK, MOBA_BLOCK)
        kb = k_ref[0, pl.ds(start, MOBA_BLOCK), :]
        vb = v_ref[0, pl.ds(start, MOBA_BLOCK), :]
        dist = (qi - n) * MOBA_BLOCK + rq - ck
        s = _dot_nt(q, kb) * SCALE - slope * dist.astype(F32)
        picked = jnp.sum(jnp.where(col == n, sel_ref[...], 0.0), axis=-1, keepdims=True) > 0.5
        ok = (picked | (n == qi)) & (dist >= 0)
        _online_update(m_ref, l_ref, acc_ref, s, ok, vb)
        return carry

    lax.fori_loop(0, qi + 1, block, 0)
    o_ref[0] = (acc_ref[...] / l_ref[...]).astype(o_ref.dtype)


def _moba_prompt(q, kv, n_heads):
    B, T, _ = q.shape
    assert T % MOBA_BLOCK == 0
    nb = T // MOBA_BLOCK
    dh = HEAD_DIM
    return pl.pallas_call(
        functools.partial(_moba_kernel, nb=nb),
        grid=(B, n_heads, nb),
        in_specs=[pl.BlockSpec(memory_space=pltpu.SMEM),
                  pl.BlockSpec((1, MOBA_BLOCK, dh), lambda b, h, i: (b, i, h)),
                  pl.BlockSpec((1, T, dh), lambda b, h, i: (b, 0, h)),
                  pl.BlockSpec((1, T, dh), lambda b, h, i: (b, 0, n_heads + h))],
        out_specs=pl.BlockSpec((1, MOBA_BLOCK, dh), lambda b, h, i: (b, i, h)),
        out_shape=jax.ShapeDtypeStruct((B, T, n_heads * dh), BF16),
        scratch_shapes=[pltpu.VMEM((MOBA_BLOCK, 1), F32), pltpu.VMEM((MOBA_BLOCK, 1), F32),
                        pltpu.VMEM((MOBA_BLOCK, dh), F32), pltpu.VMEM((MOBA_BLOCK, nb), F32)],
        compiler_params=_params("parallel", "parallel", "arbitrary"),
        name="moba_prompt",
    )(_alibi_slopes(n_heads), q, kv, kv)


def _compress_kernel(x_ref, pe_ref, w1_ref, w2_ref, o_ref, *, nc):
    x = x_ref[0, 0, 0]
    half = x.shape[1]
    lo = _dot(x + pe_ref[0, 0:1, :], w1_ref[:half, :])
    hi = _dot(x + pe_ref[0, 1:2, :], w1_ref[half:, :])
    rows = x.shape[0]
    hid = lo + pltpu.roll(hi, rows - 1, axis=0)
    y = _dot(hid * jax.nn.sigmoid(hid), w2_ref[...])
    ridx = lax.broadcasted_iota(jnp.int32, y.shape, 0)
    o_ref[0, 0, 0] = jnp.where(ridx < nc, y, 0.0)


def _compress_prompt(kv16, pe, w1, w2, layer):
    B, two, G, R, W = kv16.shape
    nc = R - 1
    hidden = w1.shape[-1]
    return pl.pallas_call(
        functools.partial(_compress_kernel, nc=nc),
        grid=(B, two, G),
        in_specs=[pl.BlockSpec((1, 1, 1, R, W), lambda b, c, g: (b, c, g, 0, 0)),
                  pl.BlockSpec((1, 2, W), lambda b, c, g: (c, 0, 0)),
                  pl.BlockSpec((None, None, 2 * W, hidden), lambda b, c, g: (layer, c, 0, 0)),
                  pl.BlockSpec((None, None, hidden, HEAD_DIM), lambda b, c, g: (layer, c, 0, 0))],
        out_specs=pl.BlockSpec((1, 1, 1, R, HEAD_DIM), lambda b, c, g: (b, c, g, 0, 0)),
        out_shape=jax.ShapeDtypeStruct((B, two, G, R, HEAD_DIM), F32),
        compiler_params=_params("parallel", "parallel", "parallel"),
        name="nsa_compress",
    )(kv16, pe, w1, w2)


def _nsa_kernel(slopes_ref, q_ref, gates_ref, kc_ref, vc_ref, ks_ref, vs_ref, kw_ref, vw_ref,
                ov_ref, ex_ref, o_ref, m_ref, l_ref, acc_ref, *, T, gs):
    g = pl.program_id(1)
    qi = pl.program_id(2)
    tq = NSA_Q_TILE
    dh = HEAD_DIM
    rows = gs * tq
    q = q_ref[0]
    qs = jnp.concatenate([q[:, s * dh:(s + 1) * dh] for s in range(gs)], axis=0)
    ridx = lax.broadcasted_iota(jnp.int32, (rows, 1), 0)
    head = ridx // tq
    pos = qi * tq + ridx % tq
    slope = jnp.zeros((rows, 1), F32)
    for s in range(gs):
        slope = jnp.where(head == s, slopes_ref[g * gs + s], slope)

    ncp = kc_ref.shape[3]
    c_end = lax.broadcasted_iota(jnp.int32, (1, ncp), 1) * CMP_STRIDE + (CMP_LEN - 1)
    dist_c = pos - c_end
    cmask = dist_c >= 0
    s_c = jnp.where(cmask, _dot_nt(qs, kc_ref[0, 0, 0]) * SCALE - slope * dist_c.astype(F32), NEG)
    e_c = jnp.where(cmask, jnp.exp(s_c - jnp.max(s_c, axis=-1, keepdims=True)), 0.0)
    den = jnp.sum(e_c, axis=-1, keepdims=True)
    p_c = e_c / jnp.where(den > 0.0, den, 1.0)
    o_cmp = _dot(p_c, vc_ref[0, 0, 0])

    imp_rows = _dot(p_c, ov_ref[...])
    imp = imp_rows[0:tq]
    for s in range(1, gs):
        imp = imp + imp_rows[s * tq:(s + 1) * tq]
    ns = imp.shape[1]
    colj = lax.broadcasted_iota(jnp.int32, (tq, ns), 1)
    cur = (qi * tq + lax.broadcasted_iota(jnp.int32, (tq, 1), 0)) // SEL_BLOCK
    sel = _topk_mask(imp, colj, cur, SEL_TOPK).astype(F32)
    sel_keys = _dot(sel, ex_ref[...])

    m_ref[...] = jnp.full(m_ref.shape, NEG, F32)
    l_ref[...] = jnp.zeros(l_ref.shape, F32)
    acc_ref[...] = jnp.zeros(acc_ref.shape, F32)
    kc = NSA_KEY_CHUNK
    for c in range(T // kc):
        @pl.when(c * kc < (qi + 1) * tq)
        def _():
            kpos = c * kc + lax.broadcasted_iota(jnp.int32, (1, kc), 1)
            dist = pos - kpos
            s = _dot_nt(qs, ks_ref[0, c * kc:(c + 1) * kc, :]) * SCALE - slope * dist.astype(F32)
            picked = jnp.concatenate([sel_keys[:, c * kc:(c + 1) * kc]] * gs, axis=0) > 0.5
            own = (kpos // SEL_BLOCK) == (pos // SEL_BLOCK)
            ok = (picked | own) & (dist >= 0)
            _online_update(m_ref, l_ref, acc_ref, s, ok, vs_ref[0, c * kc:(c + 1) * kc, :])
    o_sel = acc_ref[...] / l_ref[...]

    span = WINDOW + tq
    w0 = pl.multiple_of(jnp.maximum(qi * tq - WINDOW, 0), tq)
    kpos = w0 + lax.broadcasted_iota(jnp.int32, (1, span), 1)
    dist = pos - kpos
    ok = (dist >= 0) & (dist < WINDOW)
    s_w = jnp.where(ok, _dot_nt(qs, kw_ref[0, pl.ds(w0, span), :]) * SCALE - slope * dist.astype(F32), NEG)
    e_w = jnp.exp(s_w - jnp.max(s_w, axis=-1, keepdims=True))
    p_w = e_w / jnp.sum(e_w, axis=-1, keepdims=True)
    o_win = _dot(p_w, vw_ref[0, pl.ds(w0, span), :])

    gate = jax.nn.sigmoid(gates_ref[0, 0])
    for s in range(gs):
        r0, r1 = s * tq, (s + 1) * tq
        o = (gate[:, s:s + 1] * o_cmp[r0:r1] + gate[:, gs + s:gs + s + 1] * o_sel[r0:r1]
             + gate[:, 2 * gs + s:2 * gs + s + 1] * o_win[r0:r1])
        o_ref[0, :, s * dh:(s + 1) * dh] = o.astype(o_ref.dtype)


def _nsa_tables(T, rows_cmp):
    ns = -(-T // SEL_BLOCK)
    c_start = np.arange(rows_cmp) * CMP_STRIDE
    s_start = np.arange(ns) * SEL_BLOCK
    nc = (T - CMP_LEN) // CMP_STRIDE + 1
    overlap = ((c_start[:, None] <= s_start[None, :] + SEL_BLOCK - 1)
               & (c_start[:, None] + CMP_LEN - 1 >= s_start[None, :])
               & (np.arange(rows_cmp)[:, None] < nc))
    expand = (np.arange(T)[None, :] // SEL_BLOCK) == np.arange(ns)[:, None]
    return jnp.asarray(overlap.astype(np.float32), BF16), jnp.asarray(expand.astype(np.float32), BF16)


def _nsa_prompt(q, gates, cmp_kv, nsa_kv, win_kv, n_heads, n_groups):
    B, T, _ = q.shape
    G = n_groups
    gs = n_heads // G
    dh = HEAD_DIM
    tq = NSA_Q_TILE
    assert T % NSA_KEY_CHUNK == 0 and T >= WINDOW + tq and T % tq == 0
    R = cmp_kv.shape[3]
    overlap, expand = _nsa_tables(T, R)
    ns = overlap.shape[1]
    rows = gs * tq
    seq = lambda col: pl.BlockSpec((1, T, dh), lambda b, g, i: (b, 0, col(g)))
    return pl.pallas_call(
        functools.partial(_nsa_kernel, T=T, gs=gs),
        grid=(B, G, T // tq),
        in_specs=[pl.BlockSpec(memory_space=pltpu.SMEM),
                  pl.BlockSpec((1, tq, gs * dh), lambda b, g, i: (b, i, g)),
                  pl.BlockSpec((1, 1, tq, 3 * gs), lambda b, g, i: (b, g, i, 0)),
                  pl.BlockSpec((1, 1, 1, R, dh), lambda b, g, i: (b, 0, g, 0, 0)),
                  pl.BlockSpec((1, 1, 1, R, dh), lambda b, g, i: (b, 1, g, 0, 0)),
                  seq(lambda g: 2 * G + g), seq(lambda g: 3 * G + g),
                  seq(lambda g: g), seq(lambda g: G + g),
                  pl.BlockSpec((R, ns), lambda b, g, i: (0, 0)),
                  pl.BlockSpec((ns, T), lambda b, g, i: (0, 0))],
        out_specs=pl.BlockSpec((1, tq, gs * dh), lambda b, g, i: (b, i, g)),
        out_shape=jax.ShapeDtypeStruct((B, T, n_heads * dh), BF16),
        scratch_shapes=[pltpu.VMEM((rows, 1), F32), pltpu.VMEM((rows, 1), F32), pltpu.VMEM((rows, dh), F32)],
        compiler_params=_params("parallel", "parallel", "arbitrary"),
        name="nsa_prompt",
    )(_alibi_slopes(n_heads), q, gates, cmp_kv, cmp_kv, nsa_kv, nsa_kv, win_kv, win_kv, overlap, expand)


def _gather_pages(cache, page_table):
    g = cache[page_table]
    return g.reshape((page_table.shape[0], -1) + cache.shape[2:])


def _to_blocks(a, blk):
    B, L, X, dh = a.shape
    nb = -(-L // blk)
    a = jnp.pad(a, ((0, 0), (0, nb * blk - L), (0, 0), (0, 0)))
    return a.reshape(B, nb, blk, X, dh).transpose(0, 3, 1, 2, 4)


def _gather_blocks(blocks, idx):
    return jax.vmap(jax.vmap(lambda bl, ix: bl[ix]))(blocks, idx)


def _decode_moba(q, pos, k, v, slopes):
    B, L, H, dh = k.shape
    kb, vb = _to_blocks(k, MOBA_BLOCK), _to_blocks(v, MOBA_BLOCK)
    nb = kb.shape[2]
    kmean = jnp.mean(kb, axis=3)
    n_top = min(MOBA_TOPK, nb)
    cur = pos // MOBA_BLOCK
    gate = jnp.einsum('bthd,bhnd->bhtn', q, kmean)
    gate = jnp.where(jnp.arange(nb)[None, :] < cur, gate, NEG)
    _, top = lax.top_k(gate, n_top)
    own = jnp.full((B, H, 1, 1), cur, top.dtype)
    idx = jnp.concatenate([top, own], axis=-1)
    blk_ok = jnp.concatenate([jnp.arange(n_top) < cur, jnp.ones((1,), bool)])[None, :]
    kg, vg = _gather_blocks(kb, idx), _gather_blocks(vb, idx)
    dist = pos - (idx[..., None] * MOBA_BLOCK + jnp.arange(MOBA_BLOCK))
    s = jnp.einsum('bthd,bhtjkd->bhtjk', q, kg) * SCALE - slopes[None, :, None, None, None] * dist
    ok = blk_ok[None, None, :, :, None] & (dist >= 0)
    p = jax.nn.softmax(jnp.where(ok, s, NEG).reshape(B, H, 1, -1), axis=-1)
    p = p.reshape(B, H, 1, n_top + 1, MOBA_BLOCK)
    return jnp.einsum('bhtjk,bhtjkd->bthd', p, vg)


def _decode_compress(k, pe, w1, w2):
    B, L, G, dh = k.shape
    nc = (L - CMP_LEN) // CMP_STRIDE + 1
    idx = np.arange(nc)[:, None] * CMP_STRIDE + np.arange(CMP_LEN)[None, :]
    blk = k[:, idx] + pe[None, None, :, None, :]
    flat = jnp.moveaxis(blk, 3, 2).reshape(B, nc, G, CMP_LEN * dh)
    return jax.nn.silu(flat @ w1) @ w2


def _decode_cmp_sel(q, pos, kcmp, vcmp, ksel, vsel, slopes, n_groups):
    B, T, NH, dh = q.shape
    G = n_groups
    gs = NH // G
    L = ksel.shape[1]
    nc = kcmp.shape[1]
    ns = -(-L // SEL_BLOCK)
    c_start = np.arange(nc) * CMP_STRIDE
    s_start = np.arange(ns) * SEL_BLOCK
    overlap = ((c_start[:, None] <= s_start[None, :] + SEL_BLOCK - 1)
               & (c_start[:, None] + CMP_LEN - 1 >= s_start[None, :]))
    overlap = jnp.asarray(overlap.astype(np.float32))
    c_end = jnp.asarray(c_start + CMP_LEN - 1, dtype=jnp.int32)
    ksb, vsb = _to_blocks(ksel, SEL_BLOCK), _to_blocks(vsel, SEL_BLOCK)
    sl = slopes.reshape(G, gs)
    n_top = min(SEL_TOPK, ns)
    qg = q.reshape(B, T, G, gs, dh)
    dist_c = (pos - c_end)[None, :]
    s = jnp.einsum('btgsd,bngd->bgstn', qg, kcmp) * SCALE - sl[None, :, :, None, None] * dist_c
    cmask = dist_c >= 0
    p = jnp.where(cmask, jax.nn.softmax(jnp.where(cmask, s, NEG), axis=-1), 0.0)
    o_cmp = jnp.einsum('bgstn,bngd->btgsd', p, vcmp)
    imp = jnp.einsum('bgstn,nj->bgtj', p, overlap)
    cur = pos // SEL_BLOCK
    imp = jnp.where(jnp.arange(ns)[None, :] < cur, imp, NEG)
    _, top = lax.top_k(imp, n_top)
    own = jnp.full((B, G, T, 1), cur, top.dtype)
    idx = jnp.concatenate([top, own], axis=-1)
    blk_ok = jnp.concatenate([jnp.arange(n_top) < cur, jnp.ones((1,), bool)])[None, :]
    kg, vg = _gather_blocks(ksb, idx), _gather_blocks(vsb, idx)
    dist = pos - (idx[..., None] * SEL_BLOCK + jnp.arange(SEL_BLOCK))
    s2 = (jnp.einsum('btgsd,bgtjkd->bgstjk', qg, kg) * SCALE
          - sl[None, :, :, None, None, None] * dist[:, :, None])
    ok = (blk_ok[None, None, :, :, None] & (dist >= 0))[:, :, None]
    p2 = jax.nn.softmax(jnp.where(ok, s2, NEG).reshape(B, G, gs, T, -1), axis=-1)
    p2 = p2.reshape(B, G, gs, T, n_top + 1, SEL_BLOCK)
    o_sel = jnp.einsum('bgstjk,bgtjkd->btgsd', p2, vg)
    return o_cmp.reshape(B, T, NH, dh), o_sel.reshape(B, T, NH, dh)


def _decode_window(q, pos, k, v, k_pos, slopes, n_groups):
    B, Tq, NH, dh = q.shape
    G = n_groups
    gs = NH // G
    qg = q.reshape(B, Tq, G, gs, dh)
    dist = (pos - k_pos)[None, :]
    ok = (dist >= 0) & (dist < WINDOW) & (k_pos >= 0)[None, :]
    s = jnp.einsum('btgsd,bkgd->bgstk', qg, k) * SCALE - slopes.reshape(G, gs)[None, :, :, None, None] * dist
    p = jax.nn.softmax(jnp.where(ok, s, NEG), axis=-1)
    return jnp.einsum('bgstk,bkgd->btgsd', p, v).reshape(B, Tq, NH, dh)


def _trunk(x3, start, past, ln_g, ln_b, ffn_gate, ffn_up, ffn_down, attn_w_in, attn_w_out,
           cmp_pe, cmp_w1, cmp_w2, pool_w, pool_scale):
    B, T, D = x3.shape
    M = B * T
    depth = ffn_gate.shape[0]
    alpha = (2.0 * depth) ** 0.25
    decode = past is not None
    dh = HEAD_DIM
    n_heads = D // dh
    H_A = n_heads // 2
    NH = n_heads - H_A
    G = cmp_pe.shape[1]
    moba_w, nsa_w, kvw = H_A * dh, NH * dh, G * dh
    tm_up = min(M, 1024)
    tm_ln = min(M, 512)
    pool_w_mx = pool_w.astype(MXU_DTYPE)

    x = x3.reshape(M, D)
    xb = x.astype(BF16)
    new_moba, new_nsa, new_win, new_pool = [], [], [], []

    def ffn(x, xb, layer, sub, ln_idx):
        hidden = _ffn_up(xb, ffn_gate, ffn_up, layer, sub, tm_up)
        return _matmul_res_ln(hidden, ffn_down, (layer, sub), x, ln_g[layer, ln_idx], ln_b[layer, ln_idx],
                              alpha=alpha, coef=0.5, tm=tm_ln, tk=256)

    for layer in range(depth):
        x, xb = ffn(x, xb, layer, 0, 0)
        if layer % 2 == 0:
            a = layer // 2
            proj = functools.partial(_proj, xb, attn_w_in, a, tm=tm_up)
            q_a = proj(0, moba_w, BF16, tn=256)
            moba_kv = proj(moba_w, 2 * moba_w, F32, tn=256)
            q_b = proj(3 * moba_w, nsa_w, BF16, tn=256)
            c0 = 3 * moba_w + nsa_w
            nsa_kv = proj(c0, 4 * kvw, F32, tn=256)
            win_kv = proj(c0 + 4 * kvw, 2 * kvw, F32, tn=256)
            gates = proj(c0 + 6 * kvw, 3 * NH, F32, tn=128)
            moba_new = moba_kv.reshape(B, T, 2, H_A, dh)
            nsa_new = nsa_kv.reshape(B, T, 4, G, dh)
            win_new = win_kv.reshape(B, T, 2, G, dh)
            if not decode:
                o_a = _moba_prompt(q_a.reshape(B, T, moba_w), moba_kv.reshape(B, T, 2 * moba_w), H_A)
                kv16 = nsa_new[:, :, 0:2].transpose(0, 2, 3, 1, 4).reshape(B, 2, G, T // 16, 16 * dh)
                cmp_kv = _compress_prompt(kv16, cmp_pe[a].reshape(2, 2, 16 * dh), cmp_w1, cmp_w2, a)
                gates_g = gates.reshape(B, T, 3, G, NH // G).transpose(0, 3, 1, 2, 4).reshape(B, G, T, 3 * (NH // G))
                o_b = _nsa_prompt(q_b.reshape(B, T, nsa_w), gates_g, cmp_kv, nsa_kv.reshape(B, T, 4 * kvw),
                                  win_kv.reshape(B, T, 2 * kvw), NH, G)
                win_state = win_new[:, -min(WINDOW, T):]
                mixed = jnp.concatenate([o_a, o_b], axis=-1).reshape(M, D)
            else:
                cache_moba_kv, cache_nsa_kv, state_nsa_win, _, page_table = past
                moba_all = jnp.concatenate([_gather_pages(cache_moba_kv[a], page_table), moba_new], axis=1)
                nsa_all = jnp.concatenate([_gather_pages(cache_nsa_kv[a], page_table), nsa_new], axis=1)
                win_past = state_nsa_win[a]
                qa4 = q_a.astype(F32).reshape(B, T, H_A, dh)
                qb4 = q_b.astype(F32).reshape(B, T, NH, dh)
                o_a = _decode_moba(qa4, start, moba_all[:, :, 0], moba_all[:, :, 1], _alibi_slopes(H_A))
                sl_b = _alibi_slopes(NH)
                k_cmp = _decode_compress(nsa_all[:, :, 0], cmp_pe[a, 0], cmp_w1[a, 0], cmp_w2[a, 0])
                v_cmp = _decode_compress(nsa_all[:, :, 1], cmp_pe[a, 1], cmp_w1[a, 1], cmp_w2[a, 1])
                o_cmp, o_sel = _decode_cmp_sel(qb4, start, k_cmp, v_cmp, nsa_all[:, :, 2], nsa_all[:, :, 3], sl_b, G)
                win_all = jnp.concatenate([win_past, win_new], axis=1)
                n_buf = win_past.shape[1]
                k_pos = start - n_buf + jnp.arange(n_buf + T, dtype=jnp.int32)
                o_win = _decode_window(qb4, start, win_all[:, :, 0], win_all[:, :, 1], k_pos, sl_b, G)
                win_state = win_all[:, -n_buf:]
                gt = jax.nn.sigmoid(gates).reshape(B, T, 3, NH, 1)
                o_b = gt[:, :, 0] * o_cmp + gt[:, :, 1] * o_sel + gt[:, :, 2] * o_win
                mixed = jnp.concatenate([o_a.reshape(B, T, moba_w), o_b.reshape(B, T, nsa_w)], axis=-1)
                mixed = mixed.reshape(M, D).astype(BF16)
            new_moba.append(moba_new)
            new_nsa.append(nsa_new)
            new_win.append(win_state)
            x, xb = _matmul_res_ln(mixed, attn_w_out, (a,), x, ln_g[layer, 1], ln_b[layer, 1],
                                   alpha=alpha, coef=1.0, tm=tm_ln, tk=256)
        else:
            p = layer // 2
            x3c = x.reshape(B, T, D)
            if decode:
                prev = jnp.concatenate([past[3][p], x3c], axis=1)
                new_pool.append(prev[:, -POOL_BUF:])
                tt = T
            else:
                prev = x3c
                new_pool.append(x3c[:, -POOL_BUF:])
                tt = 256
            xo, xbo = _pool_mixer_ln(x3c, prev, pool_w_mx[p], pool_scale[p], ln_g[layer, 1], ln_b[layer, 1],
                                     alpha=alpha, start=start, tt=tt, decode=decode)
            x, xb = xo.reshape(M, D), xbo.reshape(M, D)
        x, xb = ffn(x, xb, layer, 1, 2)
    return (x.reshape(B, T, D), jnp.stack(new_moba), jnp.stack(new_nsa), jnp.stack(new_win),
            jnp.stack(new_pool))


def kernel(x_prompt, x_sample, cache_moba_kv, cache_nsa_kv, state_nsa_win, state_pool, page_table,
           ln_g, ln_b, ffn_gate, ffn_up, ffn_down, attn_w_in, attn_w_out, cmp_pe, cmp_w1, cmp_w2,
           pool_w, pool_scale):
    weights = (ln_g, ln_b, ffn_gate, ffn_up, ffn_down, attn_w_in, attn_w_out, cmp_pe, cmp_w1, cmp_w2,
               pool_w, pool_scale)
    past_len = page_table.shape[1] * cache_moba_kv.shape[2]
    y_p, moba_p, nsa_p, win_p, pool_p = _trunk(x_prompt, 0, None, *weights)
    y_s, moba_s, nsa_s, win_s, pool_s = _trunk(
        x_sample, past_len, (cache_moba_kv, cache_nsa_kv, state_nsa_win, state_pool, page_table), *weights)
    return (y_p, y_s, moba_p, nsa_p, win_p, pool_p, moba_s, nsa_s, win_s, pool_s)
```

```python
import functools

import jax
import jax.numpy as jnp
import numpy as np
from jax import lax
from jax.experimental import pallas as pl
from jax.experimental.pallas import tpu as pltpu

F32 = jnp.float32
BF16 = jnp.bfloat16
MXU_DTYPE = BF16

HEAD_DIM = 128
MOBA_BLOCK = 256
MOBA_TOPK = 3
CMP_LEN = 32
CMP_STRIDE = 16
SEL_BLOCK = 64
SEL_TOPK = 15
WINDOW = 512
POOL_WINDOWS = (2, 4, 8, 16)
POOL_BUF = max(POOL_WINDOWS) - 1
LN_EPS = 1e-5
SCALE = HEAD_DIM ** -0.5
NEG = -1e30
LOWEST = -3e38
NSA_Q_TILE = 128
NSA_KEY_CHUNK = 512

V7X_VMEM_BYTES = 64 * 1024 * 1024
VMEM_LIMIT = V7X_VMEM_BYTES - 8 * 1024 * 1024


def _params(*sem):
    return pltpu.CompilerParams(dimension_semantics=sem, vmem_limit_bytes=VMEM_LIMIT)


def _mx(a):
    return a.astype(MXU_DTYPE)


def _dot(a, b):
    return jnp.dot(_mx(a), _mx(b), preferred_element_type=F32)


def _dot_nt(a, b):
    return lax.dot_general(_mx(a), _mx(b), (((1,), (1,)), ((), ())), preferred_element_type=F32)


def _alibi_slopes(n):
    return jnp.asarray(np.exp2(-8.0 * np.arange(1, n + 1) / n), dtype=F32)


def _layernorm_rows(z, g, b):
    mu = jnp.mean(z, axis=-1, keepdims=True)
    zc = z - mu
    var = jnp.mean(zc * zc, axis=-1, keepdims=True)
    return zc * lax.rsqrt(var + LN_EPS) * g + b


def _ffn_up_kernel(x_ref, wg_ref, wu_ref, o_ref):
    x = x_ref[...]
    g = _dot(x, wg_ref[...])
    u = _dot(x, wu_ref[...])
    o_ref[...] = (g * jax.nn.sigmoid(g) * u).astype(o_ref.dtype)


def _ffn_up(xb, w_gate, w_up, layer, sub, tm, tn=256):
    M, D = xb.shape
    F = w_gate.shape[-1]
    assert M % tm == 0 and F % tn == 0
    wspec = pl.BlockSpec((None, None, D, tn), lambda i, j: (layer, sub, 0, j))
    return pl.pallas_call(
        _ffn_up_kernel,
        grid=(M // tm, F // tn),
        in_specs=[pl.BlockSpec((tm, D), lambda i, j: (i, 0)), wspec, wspec],
        out_specs=pl.BlockSpec((tm, tn), lambda i, j: (i, j)),
        out_shape=jax.ShapeDtypeStruct((M, F), BF16),
        compiler_params=_params("parallel", "arbitrary"),
        name="ffn_up",
    )(xb, w_gate, w_up)


def _matmul_kernel(x_ref, w_ref, o_ref, *, kchunk):
    K = x_ref.shape[1]
    acc = None
    for k0 in range(0, K, kchunk):
        part = _dot(x_ref[:, k0:k0 + kchunk], w_ref[k0:k0 + kchunk, :])
        acc = part if acc is None else acc + part
    o_ref[...] = acc[:, :o_ref.shape[1]].astype(o_ref.dtype)


def _matmul(xb, w, widx, col0, ncols, out_dtype, *, tm, tn, kchunk=None, name):
    M, K = xb.shape
    assert M % tm == 0 and col0 % tn == 0
    kchunk = K if kchunk is None else kchunk
    assert K % kchunk == 0
    if ncols % tn == 0:
        nj, out_tn = ncols // tn, tn
    else:
        assert ncols < tn
        nj, out_tn = 1, ncols
    j0 = col0 // tn
    lead = (None,) * len(widx)
    x_spec = pl.BlockSpec((tm, K), lambda i, j: (i, 0), pipeline_mode=pl.Buffered(1))
    return pl.pallas_call(
        functools.partial(_matmul_kernel, kchunk=kchunk),
        grid=(M // tm, nj),
        in_specs=[x_spec, pl.BlockSpec(lead + (K, tn), lambda i, j: tuple(widx) + (0, j0 + j))],
        out_specs=pl.BlockSpec((tm, out_tn), lambda i, j: (i, j)),
        out_shape=jax.ShapeDtypeStruct((M, ncols), out_dtype),
        compiler_params=_params("parallel", "arbitrary"),
        name=name,
    )(xb, w)


def _res_ln_kernel(y_ref, res_ref, g_ref, b_ref, o_ref, ob_ref, *, alpha, coef):
    z = alpha * res_ref[...] + coef * y_ref[...]
    y = _layernorm_rows(z, g_ref[...], b_ref[...])
    o_ref[...] = y
    ob_ref[...] = y.astype(ob_ref.dtype)


def _res_ln(y, res, g, b, *, alpha, coef, tm):
    M, D = y.shape
    assert M % tm == 0
    row = pl.BlockSpec((1, D), lambda i: (0, 0))
    tile = pl.BlockSpec((tm, D), lambda i: (i, 0))
    return pl.pallas_call(
        functools.partial(_res_ln_kernel, alpha=alpha, coef=coef),
        grid=(M // tm,),
        in_specs=[tile, tile, row, row],
        out_specs=[tile, tile],
        out_shape=[jax.ShapeDtypeStruct((M, D), F32), jax.ShapeDtypeStruct((M, D), BF16)],
        compiler_params=_params("parallel"),
        name="res_ln",
    )(y, res, g.reshape(1, D), b.reshape(1, D))


def _pool_kernel(x_ref, prev_ref, w_ref, sc_ref, g_ref, b_ref, o_ref, ob_ref, *,
                 alpha, start, tt, decode, group):
    ti = pl.program_id(1)
    zs = []
    for gi, win in enumerate(POOL_WINDOWS):
        c0, c1 = gi * group, (gi + 1) * group
        prev = prev_ref[0, :, c0:c1]
        if decode:
            xg = prev[POOL_BUF:POOL_BUF + 1]
            wsum = jnp.sum(prev[POOL_BUF + 1 - win:], axis=0, keepdims=True)
            cnt = jnp.full((1, 1), float(min(win, start + 1)), F32)
        else:
            xg = x_ref[0, :, c0:c1]
            prev = jnp.where(ti == 0, 0.0, prev)
            s = jnp.concatenate([prev, xg], axis=0)
            span = 1
            while span < win:
                s = s + pltpu.roll(s, span, axis=0)
                span *= 2
            wsum = s[POOL_BUF + 1:]
            pos = start + ti * tt + lax.broadcasted_iota(jnp.int32, (tt, 1), 0)
            cnt = jnp.minimum(win, pos + 1).astype(F32)
        pooled = wsum / cnt - xg
        y = _dot(pooled, w_ref[gi]) * sc_ref[:, c0:c1]
        zs.append(alpha * xg + y)
    d = group * len(POOL_WINDOWS)
    mu = sum(jnp.sum(z, axis=-1, keepdims=True) for z in zs) / d
    var = sum(jnp.sum((z - mu) * (z - mu), axis=-1, keepdims=True) for z in zs) / d
    inv = lax.rsqrt(var + LN_EPS)
    for gi, z in enumerate(zs):
        c0, c1 = gi * group, (gi + 1) * group
        y = (z - mu) * inv * g_ref[:, c0:c1] + b_ref[:, c0:c1]
        o_ref[0, :, c0:c1] = y
        ob_ref[0, :, c0:c1] = y.astype(ob_ref.dtype)


def _pool_mixer_ln(x3, prev3, w_pool, scale, g, b, *, alpha, start, tt, decode):
    B, T, D = x3.shape
    group = D // len(POOL_WINDOWS)
    assert T % tt == 0 and (decode or tt % 16 == 0)
    per = tt // 16 if not decode else 0
    if decode:
        prev_map = lambda bi, ti: (bi, 0, 0)
    else:
        prev_map = lambda bi, ti: (bi, jnp.maximum(ti * per - 1, 0), 0)
    row = pl.BlockSpec((1, D), lambda bi, ti: (0, 0))
    tile = pl.BlockSpec((1, tt, D), lambda bi, ti: (bi, ti, 0))
    return pl.pallas_call(
        functools.partial(_pool_kernel, alpha=alpha, start=start, tt=tt, decode=decode, group=group),
        grid=(B, T // tt),
        in_specs=[tile, pl.BlockSpec((1, 16, D), prev_map),
                  pl.BlockSpec(w_pool.shape, lambda bi, ti: (0, 0, 0)), row, row, row],
        out_specs=[tile, tile],
        out_shape=[jax.ShapeDtypeStruct((B, T, D), F32), jax.ShapeDtypeStruct((B, T, D), BF16)],
        compiler_params=_params("parallel", "arbitrary"),
        name="pool_mixer_ln",
    )(x3, prev3, w_pool, scale.reshape(1, D), g.reshape(1, D), b.reshape(1, D))


def _topk_mask(score, col, n_valid, k):
    n = score.shape[1]
    masked = jnp.where(col < n_valid, score, NEG)
    rank = jnp.zeros(score.shape, jnp.int32)
    for j in range(n):
        other = masked[:, j:j + 1]
        ahead = (other > masked) | ((other == masked) & (j < col))
        rank = rank + ahead.astype(jnp.int32)
    return (rank < k) & (col < n_valid)


def _online_update(m_ref, l_ref, acc_ref, s, v):
    m_old = m_ref[...]
    m_new = jnp.maximum(m_old, jnp.max(s, axis=-1, keepdims=True))
    p = jnp.exp(s - m_new)
    a = jnp.exp(m_old - m_new)
    l_ref[...] = a * l_ref[...] + jnp.sum(p, axis=-1, keepdims=True)
    acc_ref[...] = a * acc_ref[...] + _dot(p, v)
    m_ref[...] = m_new


def _moba_kernel(slopes_ref, q_ref, k_ref, v_ref, o_ref, m_ref, l_ref, acc_ref, sel_ref, *, nb):
    h = pl.program_id(1)
    qi = pl.program_id(2)
    blk = MOBA_BLOCK
    slope = slopes_ref[h]
    q = q_ref[0]
    kmean = jnp.concatenate(
        [jnp.sum(k_ref[0, n * blk:(n + 1) * blk, :], axis=0, keepdims=True) for n in range(nb)],
        axis=0) * (1.0 / blk)
    gate = _dot_nt(q, kmean)
    col = lax.broadcasted_iota(jnp.int32, gate.shape, 1)
    sel_ref[...] = _topk_mask(gate, col, qi, MOBA_TOPK).astype(F32)

    rel = lax.broadcasted_iota(jnp.int32, (blk, blk), 0) - lax.broadcasted_iota(jnp.int32, (blk, blk), 1)
    alibi = slope * rel.astype(F32)

    own = pl.multiple_of(qi * blk, blk)
    s = _dot_nt(q, k_ref[0, pl.ds(own, blk), :]) * SCALE - alibi + jnp.where(rel >= 0, 0.0, NEG)
    m = jnp.max(s, axis=-1, keepdims=True)
    p = jnp.exp(s - m)
    m_ref[...] = m
    l_ref[...] = jnp.sum(p, axis=-1, keepdims=True)
    acc_ref[...] = _dot(p, v_ref[0, pl.ds(own, blk), :])

    def block(n, carry):
        start = pl.multiple_of(n * blk, blk)
        picked = jnp.sum(jnp.where(col == n, sel_ref[...], 0.0), axis=-1, keepdims=True) > 0.5
        row_term = jnp.where(picked, 0.0, NEG) - slope * ((qi - n) * blk).astype(F32)
        s = _dot_nt(q, k_ref[0, pl.ds(start, blk), :]) * SCALE - alibi + row_term
        _online_update(m_ref, l_ref, acc_ref, s, v_ref[0, pl.ds(start, blk), :])
        return carry

    lax.fori_loop(0, qi, block, 0)
    o_ref[0] = (acc_ref[...] / l_ref[...]).astype(o_ref.dtype)


def _moba_prompt(q, kv, n_heads):
    B, T, _ = q.shape
    assert T % MOBA_BLOCK == 0
    nb = T // MOBA_BLOCK
    dh = HEAD_DIM
    return pl.pallas_call(
        functools.partial(_moba_kernel, nb=nb),
        grid=(B, n_heads, nb),
        in_specs=[pl.BlockSpec(memory_space=pltpu.SMEM),
                  pl.BlockSpec((1, MOBA_BLOCK, dh), lambda b, h, i: (b, i, h)),
                  pl.BlockSpec((1, T, dh), lambda b, h, i: (b, 0, h)),
                  pl.BlockSpec((1, T, dh), lambda b, h, i: (b, 0, n_heads + h))],
        out_specs=pl.BlockSpec((1, MOBA_BLOCK, dh), lambda b, h, i: (b, i, h)),
        out_shape=jax.ShapeDtypeStruct((B, T, n_heads * dh), BF16),
        scratch_shapes=[pltpu.VMEM((MOBA_BLOCK, 1), F32), pltpu.VMEM((MOBA_BLOCK, 1), F32),
                        pltpu.VMEM((MOBA_BLOCK, dh), F32), pltpu.VMEM((MOBA_BLOCK, nb), F32)],
        compiler_params=_params("parallel", "parallel", "arbitrary"),
        name="moba_prompt",
    )(_alibi_slopes(n_heads), q, kv, kv)


def _compress_kernel(x_ref, pe_ref, w1_ref, w2_ref, o_ref, *, nc):
    x = x_ref[0, 0, 0]
    half = x.shape[1]
    lo = _dot(x + pe_ref[0, 0:1, :], w1_ref[:half, :])
    hi = _dot(x + pe_ref[0, 1:2, :], w1_ref[half:, :])
    rows = x.shape[0]
    hid = lo + pltpu.roll(hi, rows - 1, axis=0)
    y = _dot(hid * jax.nn.sigmoid(hid), w2_ref[...])
    ridx = lax.broadcasted_iota(jnp.int32, y.shape, 0)
    o_ref[0, 0, 0] = jnp.where(ridx < nc, y, 0.0)


def _compress_prompt(kv16, pe, w1, w2, layer):
    B, two, G, R, W = kv16.shape
    nc = R - 1
    hidden = w1.shape[-1]
    return pl.pallas_call(
        functools.partial(_compress_kernel, nc=nc),
        grid=(B, two, G),
        in_specs=[pl.BlockSpec((1, 1, 1, R, W), lambda b, c, g: (b, c, g, 0, 0)),
                  pl.BlockSpec((1, 2, W), lambda b, c, g: (c, 0, 0)),
                  pl.BlockSpec((None, None, 2 * W, hidden), lambda b, c, g: (layer, c, 0, 0)),
                  pl.BlockSpec((None, None, hidden, HEAD_DIM), lambda b, c, g: (layer, c, 0, 0))],
        out_specs=pl.BlockSpec((1, 1, 1, R, HEAD_DIM), lambda b, c, g: (b, c, g, 0, 0)),
        out_shape=jax.ShapeDtypeStruct((B, two, G, R, HEAD_DIM), F32),
        compiler_params=_params("parallel", "parallel", "parallel"),
        name="nsa_compress",
    )(kv16, pe, w1, w2)


def _nsa_kernel(slopes_ref, q_ref, gates_ref, kc_ref, vc_ref, ks_ref, vs_ref, kw_ref, vw_ref,
                ov_ref, ex_ref, o_ref, m_ref, l_ref, acc_ref, *, T, gs):
    g = pl.program_id(1)
    qi = pl.program_id(2)
    tq = NSA_Q_TILE
    dh = HEAD_DIM
    q = q_ref[0]
    qs = jnp.concatenate([q[:, s * dh:(s + 1) * dh] for s in range(gs)], axis=0)
    slopes = [slopes_ref[g * gs + s] for s in range(gs)]
    heads = [(s * tq, (s + 1) * tq) for s in range(gs)]
    tpos = qi * tq + lax.broadcasted_iota(jnp.int32, (tq, 1), 0)

    ncp = kc_ref.shape[3]
    c_end = lax.broadcasted_iota(jnp.int32, (1, ncp), 1) * CMP_STRIDE + (CMP_LEN - 1)
    dist_c = tpos - c_end
    cmask = dist_c >= 0
    bias_c = jnp.where(cmask, 0.0, NEG)
    distf_c = dist_c.astype(F32)
    s_all = _dot_nt(qs, kc_ref[0, 0, 0])
    p_list = []
    for s, (r0, r1) in enumerate(heads):
        sc = s_all[r0:r1] * SCALE - slopes[s] * distf_c + bias_c
        e = jnp.where(cmask, jnp.exp(sc - jnp.max(sc, axis=-1, keepdims=True)), 0.0)
        den = jnp.sum(e, axis=-1, keepdims=True)
        p_list.append(e / jnp.where(den > 0.0, den, 1.0))
    p_c = _mx(jnp.concatenate(p_list, axis=0))
    o_cmp = _dot(p_c, vc_ref[0, 0, 0])

    imp_rows = _dot(p_c, ov_ref[...])
    imp = imp_rows[0:tq]
    for r0, r1 in heads[1:]:
        imp = imp + imp_rows[r0:r1]
    ns = imp.shape[1]
    colj = lax.broadcasted_iota(jnp.int32, (tq, ns), 1)
    sel = _topk_mask(imp, colj, tpos // SEL_BLOCK, SEL_TOPK).astype(F32)
    sel_keys = _dot(sel, ex_ref[...])

    m_ref[...] = jnp.full(m_ref.shape, NEG, F32)
    l_ref[...] = jnp.zeros(l_ref.shape, F32)
    acc_ref[...] = jnp.zeros(acc_ref.shape, F32)
    kc = NSA_KEY_CHUNK
    for c in reversed(range(T // kc)):
        @pl.when(c * kc < (qi + 1) * tq)
        def _():
            kpos = c * kc + lax.broadcasted_iota(jnp.int32, (1, kc), 1)
            dist = tpos - kpos
            own = (kpos // SEL_BLOCK) == (tpos // SEL_BLOCK)
            ok = ((sel_keys[:, c * kc:(c + 1) * kc] > 0.5) | own) & (dist >= 0)
            bias = jnp.where(ok, 0.0, NEG)
            distf = dist.astype(F32)
            s_all = _dot_nt(qs, ks_ref[0, c * kc:(c + 1) * kc, :])
            ps, scales = [], []
            for s, (r0, r1) in enumerate(heads):
                sc = s_all[r0:r1] * SCALE - slopes[s] * distf + bias
                m_old = m_ref[r0:r1]
                m_new = jnp.maximum(m_old, jnp.max(sc, axis=-1, keepdims=True))
                p = jnp.exp(sc - m_new)
                a = jnp.exp(m_old - m_new)
                l_ref[r0:r1] = a * l_ref[r0:r1] + jnp.sum(p, axis=-1, keepdims=True)
                m_ref[r0:r1] = m_new
                ps.append(_mx(p))
                scales.append(a)
            pv = _dot(jnp.concatenate(ps, axis=0), vs_ref[0, c * kc:(c + 1) * kc, :])
            acc_ref[...] = jnp.concatenate(scales, axis=0) * acc_ref[...] + pv
    o_sel = acc_ref[...] / l_ref[...]

    span = WINDOW + tq
    w0 = pl.multiple_of(jnp.maximum(qi * tq - WINDOW, 0), tq)
    kpos = w0 + lax.broadcasted_iota(jnp.int32, (1, span), 1)
    dist = tpos - kpos
    bias = jnp.where((dist >= 0) & (dist < WINDOW), 0.0, NEG)
    distf = dist.astype(F32)
    s_all = _dot_nt(qs, kw_ref[0, pl.ds(w0, span), :])
    ps = []
    for s, (r0, r1) in enumerate(heads):
        sc = s_all[r0:r1] * SCALE - slopes[s] * distf + bias
        e = jnp.exp(sc - jnp.max(sc, axis=-1, keepdims=True))
        ps.append(_mx(e / jnp.sum(e, axis=-1, keepdims=True)))
    o_win = _dot(jnp.concatenate(ps, axis=0), vw_ref[0, pl.ds(w0, span), :])

    gate = jax.nn.sigmoid(gates_ref[0, 0])
    for s, (r0, r1) in enumerate(heads):
        o = (gate[:, s:s + 1] * o_cmp[r0:r1] + gate[:, gs + s:gs + s + 1] * o_sel[r0:r1]
             + gate[:, 2 * gs + s:2 * gs + s + 1] * o_win[r0:r1])
        o_ref[0, :, s * dh:(s + 1) * dh] = o.astype(o_ref.dtype)


def _nsa_tables(T, rows_cmp):
    ns = -(-T // SEL_BLOCK)
    c_start = np.arange(rows_cmp) * CMP_STRIDE
    s_start = np.arange(ns) * SEL_BLOCK
    nc = (T - CMP_LEN) // CMP_STRIDE + 1
    overlap = ((c_start[:, None] <= s_start[None, :] + SEL_BLOCK - 1)
               & (c_start[:, None] + CMP_LEN - 1 >= s_start[None, :])
               & (np.arange(rows_cmp)[:, None] < nc))
    expand = (np.arange(T)[None, :] // SEL_BLOCK) == np.arange(ns)[:, None]
    return jnp.asarray(overlap.astype(np.float32), BF16), jnp.asarray(expand.astype(np.float32), BF16)


def _nsa_prompt(q, gates, cmp_kv, nsa_kv, win_kv, n_heads, n_groups):
    B, T, _ = q.shape
    G = n_groups
    gs = n_heads // G
    dh = HEAD_DIM
    tq = NSA_Q_TILE
    assert T % NSA_KEY_CHUNK == 0 and T >= WINDOW + tq and T % tq == 0
    R = cmp_kv.shape[3]
    overlap, expand = _nsa_tables(T, R)
    ns = overlap.shape[1]
    rows = gs * tq
    seq = lambda col: pl.BlockSpec((1, T, dh), lambda b, g, i: (b, 0, col(g)))
    return pl.pallas_call(
        functools.partial(_nsa_kernel, T=T, gs=gs),
        grid=(B, G, T // tq),
        in_specs=[pl.BlockSpec(memory_space=pltpu.SMEM),
                  pl.BlockSpec((1, tq, gs * dh), lambda b, g, i: (b, i, g)),
                  pl.BlockSpec((1, 1, tq, 3 * gs), lambda b, g, i: (b, g, i, 0)),
                  pl.BlockSpec((1, 1, 1, R, dh), lambda b, g, i: (b, 0, g, 0, 0)),
                  pl.BlockSpec((1, 1, 1, R, dh), lambda b, g, i: (b, 1, g, 0, 0)),
                  seq(lambda g: 2 * G + g), seq(lambda g: 3 * G + g),
                  seq(lambda g: g), seq(lambda g: G + g),
                  pl.BlockSpec((R, ns), lambda b, g, i: (0, 0)),
                  pl.BlockSpec((ns, T), lambda b, g, i: (0, 0))],
        out_specs=pl.BlockSpec((1, tq, gs * dh), lambda b, g, i: (b, i, g)),
        out_shape=jax.ShapeDtypeStruct((B, T, n_heads * dh), BF16),
        scratch_shapes=[pltpu.VMEM((rows, 1), F32), pltpu.VMEM((rows, 1), F32), pltpu.VMEM((rows, dh), F32)],
        compiler_params=_params("parallel", "parallel", "arbitrary"),
        name="nsa_prompt",
    )(_alibi_slopes(n_heads), q, gates, cmp_kv, cmp_kv, nsa_kv, nsa_kv, win_kv, win_kv, overlap, expand)


def _first_argmax(score, lane):
    mx = jnp.max(score, axis=-1, keepdims=True)
    return jnp.min(jnp.where(score == mx, lane, float(score.shape[1])), axis=-1, keepdims=True)


def _topk_indices(score, k, width):
    lane = lax.broadcasted_iota(jnp.int32, score.shape, 1).astype(F32)
    out_lane = lax.broadcasted_iota(jnp.int32, (score.shape[0], width), 1)
    out = jnp.zeros((score.shape[0], width), jnp.int32)
    for j in range(k):
        idx = _first_argmax(score, lane)
        out = jnp.where(out_lane == j, idx.astype(jnp.int32), out)
        score = jnp.where(lane == idx, LOWEST, score)
    return out


def _moba_select_kernel(pt_ref, q_ref, cache_ref, idx_ref, ksum_ref, *, n_heads, n_pages, pages_per_block):
    p = pl.program_id(1)

    @pl.when(p == 0)
    def _():
        ksum_ref[...] = jnp.zeros(ksum_ref.shape, F32)

    row = p // pages_per_block
    ksum_ref[pl.ds(row, 1), :] += jnp.sum(cache_ref[...], axis=0, keepdims=True)

    @pl.when(p == n_pages - 1)
    def _():
        dh = HEAD_DIM
        kmean = ksum_ref[...] * (1.0 / MOBA_BLOCK)
        q = q_ref[0]
        gate = jnp.concatenate(
            [_dot_nt(q[:, h * dh:(h + 1) * dh], kmean[:, h * dh:(h + 1) * dh]) for h in range(n_heads)], axis=0)
        idx_ref[0] = _topk_indices(gate, MOBA_TOPK, idx_ref.shape[2])


def _moba_decode_select(q, cache, layer, page_table, n_heads):
    B, n_pages = page_table.shape
    page = cache.shape[2]
    width = n_heads * HEAD_DIM
    ppb = MOBA_BLOCK // page
    assert MOBA_BLOCK % page == 0 and n_pages % ppb == 0 and n_pages // ppb >= MOBA_TOPK
    grid_spec = pltpu.PrefetchScalarGridSpec(
        num_scalar_prefetch=1,
        grid=(B, n_pages),
        in_specs=[pl.BlockSpec((1, 1, width), lambda b, p, pt: (b, 0, 0)),
                  pl.BlockSpec((None, None, page, width), lambda b, p, pt: (layer, pt[b * n_pages + p], 0, 0))],
        out_specs=pl.BlockSpec((1, n_heads, 128), lambda b, p, pt: (b, 0, 0)),
        scratch_shapes=[pltpu.VMEM((n_pages // ppb, width), F32)])
    return pl.pallas_call(
        functools.partial(_moba_select_kernel, n_heads=n_heads, n_pages=n_pages, pages_per_block=ppb),
        grid_spec=grid_spec,
        out_shape=jax.ShapeDtypeStruct((B, n_heads, 128), jnp.int32),
        compiler_params=_params("parallel", "arbitrary"),
        name="moba_decode_select",
    )(page_table.reshape(-1), q, cache)


def _moba_attend_kernel(phys_ref, blk_ref, slopes_ref, q_ref, kn_ref, vn_ref, *rest, n_heads, pos, page, ppb):
    cache_refs, o_ref = rest[:-1], rest[-1]
    b = pl.program_id(0)
    h = pl.program_id(1)
    slope = slopes_ref[h]
    q = q_ref[0]
    n_sel = len(cache_refs) // (2 * ppb)
    lane = lax.broadcasted_iota(jnp.int32, (1, page), 1)
    scores, values = [], []
    for j in range(n_sel):
        blk = blk_ref[(b * n_heads + h) * n_sel + j]
        for r in range(ppb):
            k_ref, v_ref = cache_refs[2 * (j * ppb + r)], cache_refs[2 * (j * ppb + r) + 1]
            dist = pos - (blk * MOBA_BLOCK + r * page + lane)
            scores.append(_dot_nt(q, k_ref[...]) * SCALE - slope * dist.astype(F32))
            values.append(v_ref)
    s_new = jnp.sum(_mx(q).astype(F32) * _mx(kn_ref[0]).astype(F32), axis=-1, keepdims=True) * SCALE
    m = s_new
    for s in scores:
        m = jnp.maximum(m, jnp.max(s, axis=-1, keepdims=True))
    p_new = jnp.exp(s_new - m)
    den = p_new
    acc = _mx(p_new).astype(F32) * _mx(vn_ref[0]).astype(F32)
    for s, v_ref in zip(scores, values):
        p = jnp.exp(s - m)
        den = den + jnp.sum(p, axis=-1, keepdims=True)
        acc = acc + _dot(p, v_ref[...])
    o_ref[0] = (acc / den).astype(o_ref.dtype)


def _moba_decode_attend(q, kv_new, cache, layer, page_table, top_idx, n_heads, pos):
    B, n_pages = page_table.shape
    page = cache.shape[2]
    dh = HEAD_DIM
    ppb = MOBA_BLOCK // page
    blk = top_idx[:, :, :MOBA_TOPK]
    pages = blk[..., None] * ppb + jnp.arange(ppb, dtype=jnp.int32)
    phys = jnp.take_along_axis(page_table[:, None, :], pages.reshape(B, 1, -1), axis=2)
    n_sel = MOBA_TOPK

    def cache_spec(j, r, col0):
        def imap(b, h, phys_ref, blk_ref):
            return (layer, phys_ref[((b * n_heads + h) * n_sel + j) * ppb + r], 0, col0 + h)
        return pl.BlockSpec((None, None, page, dh), imap)

    cache_specs = []
    for j in range(n_sel):
        for r in range(ppb):
            cache_specs += [cache_spec(j, r, 0), cache_spec(j, r, n_heads)]
    tok = lambda col0: pl.BlockSpec((1, 1, dh), lambda b, h, *_: (b, 0, col0 + h))
    grid_spec = pltpu.PrefetchScalarGridSpec(
        num_scalar_prefetch=2,
        grid=(B, n_heads),
        in_specs=[pl.BlockSpec(memory_space=pltpu.SMEM), tok(0), tok(0), tok(n_heads)] + cache_specs,
        out_specs=tok(0))
    return pl.pallas_call(
        functools.partial(_moba_attend_kernel, n_heads=n_heads, pos=pos, page=page, ppb=ppb),
        grid_spec=grid_spec,
        out_shape=jax.ShapeDtypeStruct((B, 1, n_heads * dh), BF16),
        compiler_params=_params("parallel", "arbitrary"),
        name="moba_decode_attend",
    )(phys.reshape(-1), blk.reshape(-1), _alibi_slopes(n_heads), q, kv_new, kv_new,
      *([cache] * (2 * n_sel * ppb)))


def _nsa_decode_compress_kernel(pt_ref, cache_ref, pe_ref, w1_ref, w2_ref, o_ref, buf_ref, *, n_pages, page, n_groups):
    p = pl.program_id(1)
    dh = HEAD_DIM
    for c in range(2 * n_groups):
        buf_ref[c, pl.ds(pl.multiple_of(p * page, page), page), :] = cache_ref[:, c * dh:(c + 1) * dh]

    @pl.when(p == n_pages - 1)
    def _():
        rows = buf_ref.shape[1] // CMP_STRIDE
        halves = CMP_LEN // CMP_STRIDE
        for c in range(2 * n_groups):
            which = c // n_groups
            parts = [None] * halves
            for l in range(CMP_STRIDE):
                x = buf_ref.at[c][pl.ds(l, rows, stride=CMP_STRIDE), :]
                for hf in range(halves):
                    ll = hf * CMP_STRIDE + l
                    d = _dot(x + pe_ref[which, ll:ll + 1, :], w1_ref[which, ll * dh:(ll + 1) * dh, :])
                    parts[hf] = d if parts[hf] is None else parts[hf] + d
            hid = parts[0]
            for hf in range(1, halves):
                hid = hid + pltpu.roll(parts[hf], rows - hf, axis=0)
            y = _dot(hid * jax.nn.sigmoid(hid), w2_ref[which])
            ridx = lax.broadcasted_iota(jnp.int32, y.shape, 0)
            o_ref[0, c] = jnp.where(ridx < rows - (halves - 1), y, 0.0)


def _nsa_decode_compress(cache, layer, page_table, pe, w1, w2, n_groups):
    B, n_pages = page_table.shape
    page = cache.shape[2]
    dh = HEAD_DIM
    width = 2 * n_groups * dh
    past = n_pages * page
    rows = past // CMP_STRIDE
    hidden = w1.shape[-1]
    grid_spec = pltpu.PrefetchScalarGridSpec(
        num_scalar_prefetch=1,
        grid=(B, n_pages),
        in_specs=[pl.BlockSpec((None, None, page, width), lambda b, p, pt: (layer, pt[b * n_pages + p], 0, 0)),
                  pl.BlockSpec((None, 2, CMP_LEN, dh), lambda b, p, pt: (layer, 0, 0, 0)),
                  pl.BlockSpec((None, 2, CMP_LEN * dh, hidden), lambda b, p, pt: (layer, 0, 0, 0)),
                  pl.BlockSpec((None, 2, hidden, dh), lambda b, p, pt: (layer, 0, 0, 0))],
        out_specs=pl.BlockSpec((1, 2 * n_groups, rows, dh), lambda b, p, pt: (b, 0, 0, 0)),
        scratch_shapes=[pltpu.VMEM((2 * n_groups, past, dh), F32)])
    return pl.pallas_call(
        functools.partial(_nsa_decode_compress_kernel, n_pages=n_pages, page=page, n_groups=n_groups),
        grid_spec=grid_spec,
        out_shape=jax.ShapeDtypeStruct((B, 2 * n_groups, rows, dh), F32),
        compiler_params=_params("parallel", "arbitrary"),
        name="nsa_decode_compress",
    )(page_table.reshape(-1), cache, pe, w1, w2)


def _split_heads(q, gs):
    dh = HEAD_DIM
    return jnp.concatenate([q[:, s * dh:(s + 1) * dh] for s in range(gs)], axis=0)


def _group_slopes(slopes_ref, g, gs):
    sub = lax.broadcasted_iota(jnp.int32, (gs, 1), 0)
    slope = jnp.zeros((gs, 1), F32)
    for s in range(gs):
        slope = jnp.where(sub == s, slopes_ref[g * gs + s], slope)
    return slope


def _nsa_select_kernel(slopes_ref, q_ref, kc_ref, vc_ref, ov_ref, ocmp_ref, idx_ref, *, gs, pos):
    g = pl.program_id(1)
    qs = _split_heads(q_ref[0], gs)
    slope = _group_slopes(slopes_ref, g, gs)
    ncp = kc_ref.shape[2]
    c_end = lax.broadcasted_iota(jnp.int32, (1, ncp), 1) * CMP_STRIDE + (CMP_LEN - 1)
    dist_c = pos - c_end
    cmask = dist_c >= 0
    s_c = jnp.where(cmask, _dot_nt(qs, kc_ref[0, 0]) * SCALE - slope * dist_c.astype(F32), NEG)
    e_c = jnp.where(cmask, jnp.exp(s_c - jnp.max(s_c, axis=-1, keepdims=True)), 0.0)
    den = jnp.sum(e_c, axis=-1, keepdims=True)
    p_c = e_c / jnp.where(den > 0.0, den, 1.0)
    ocmp_ref[0, 0] = _dot(p_c, vc_ref[0, 0])
    imp = jnp.sum(_dot(p_c, ov_ref[...]), axis=0, keepdims=True)
    idx = _topk_indices(imp, SEL_TOPK, idx_ref.shape[3])
    idx_ref[0, 0] = jnp.broadcast_to(idx, idx_ref.shape[2:])


def _nsa_decode_select(q, cmp_kv, n_heads, n_groups, pos):
    B = q.shape[0]
    G = n_groups
    gs = n_heads // G
    dh = HEAD_DIM
    R = cmp_kv.shape[2]
    cur = pos // SEL_BLOCK
    assert pos % SEL_BLOCK == 0 and cur >= SEL_TOPK
    c_start = np.arange(R) * CMP_STRIDE
    s_start = np.arange(cur) * SEL_BLOCK
    nc = (pos + 1 - CMP_LEN) // CMP_STRIDE + 1
    overlap = ((c_start[:, None] <= s_start[None, :] + SEL_BLOCK - 1)
               & (c_start[:, None] + CMP_LEN - 1 >= s_start[None, :])
               & (np.arange(R)[:, None] < nc))
    overlap = jnp.asarray(overlap.astype(np.float32), BF16)
    return pl.pallas_call(
        functools.partial(_nsa_select_kernel, gs=gs, pos=pos),
        grid=(B, G),
        in_specs=[pl.BlockSpec(memory_space=pltpu.SMEM),
                  pl.BlockSpec((1, 1, gs * dh), lambda b, g: (b, 0, g)),
                  pl.BlockSpec((1, 1, R, dh), lambda b, g: (b, g, 0, 0)),
                  pl.BlockSpec((1, 1, R, dh), lambda b, g: (b, G + g, 0, 0)),
                  pl.BlockSpec((R, cur), lambda b, g: (0, 0))],
        out_specs=[pl.BlockSpec((1, 1, gs, dh), lambda b, g: (b, g, 0, 0)),
                   pl.BlockSpec((1, 1, 8, 128), lambda b, g: (b, g, 0, 0))],
        out_shape=[jax.ShapeDtypeStruct((B, G, gs, dh), F32), jax.ShapeDtypeStruct((B, G, 8, 128), jnp.int32)],
        compiler_params=_params("parallel", "parallel"),
        name="nsa_decode_select",
    )(_alibi_slopes(n_heads), q, cmp_kv, cmp_kv, overlap)


def _nsa_attend_kernel(pool_blk_ref, blk_ref, slopes_ref, q_ref, gates_ref, ocmp_ref, ksn_ref, vsn_ref,
                       kwn_ref, vwn_ref, kw_ref, vw_ref, *rest, n_groups, gs, pos):
    cache_refs, o_ref = rest[:-1], rest[-1]
    b = pl.program_id(0)
    g = pl.program_id(1)
    dh = HEAD_DIM
    n_sel = len(cache_refs) // 2
    qs = _split_heads(q_ref[0], gs)
    slope = _group_slopes(slopes_ref, g, gs)

    def attend(k_past, v_past, dist, ok, k_new, v_new):
        s = jnp.where(ok, _dot_nt(qs, k_past) * SCALE - slope * dist.astype(F32), NEG)
        s_new = _dot_nt(qs, jnp.broadcast_to(k_new, (gs, dh)))[:, 0:1] * SCALE
        m = jnp.maximum(jnp.max(s, axis=-1, keepdims=True), s_new)
        p = jnp.exp(s - m)
        p_new = jnp.exp(s_new - m)
        den = jnp.sum(p, axis=-1, keepdims=True) + p_new
        acc = _dot(p, v_past) + _mx(p_new).astype(F32) * _mx(v_new).astype(F32)
        return acc / den

    lane = lax.broadcasted_iota(jnp.int32, (1, SEL_BLOCK), 1)
    dists = [pos - (blk_ref[(b * n_groups + g) * n_sel + j] * SEL_BLOCK + lane) for j in range(n_sel)]
    dist = jnp.concatenate(dists, axis=1)
    k_sel = jnp.concatenate([cache_refs[2 * j][...] for j in range(n_sel)], axis=0)
    v_sel = jnp.concatenate([cache_refs[2 * j + 1][...] for j in range(n_sel)], axis=0)
    o_sel = attend(k_sel, v_sel, dist, dist >= 0, ksn_ref[0], vsn_ref[0])

    n_buf = kw_ref.shape[0]
    dist_w = n_buf - lax.broadcasted_iota(jnp.int32, (1, n_buf), 1)
    o_win = attend(kw_ref[...], vw_ref[...], dist_w, dist_w < WINDOW, kwn_ref[0], vwn_ref[0])

    gate = jax.nn.sigmoid(gates_ref[0, 0])
    o = gate[:, 0:1] * ocmp_ref[0, 0] + gate[:, 1:2] * o_sel + gate[:, 2:3] * o_win
    for s in range(gs):
        o_ref[0, :, s * dh:(s + 1) * dh] = o[s:s + 1].astype(o_ref.dtype)


def _nsa_decode_attend(q, gates, o_cmp, nsa_new, win_new, cache, win_state, layer, page_table, top_idx,
                       n_heads, n_groups, pos):
    B, n_pages = page_table.shape
    page = cache.shape[2]
    G = n_groups
    gs = n_heads // G
    dh = HEAD_DIM
    per_page = page // SEL_BLOCK
    n_buf = win_state.shape[2]
    assert page % SEL_BLOCK == 0 and pos >= n_buf
    blk = top_idx[:, :, 0, :SEL_TOPK]
    phys = jnp.take_along_axis(page_table[:, None, :], (blk // per_page).reshape(B, 1, -1), axis=2)
    pool_blk = phys.reshape(B, G, SEL_TOPK) * per_page + blk % per_page
    n_sel = SEL_TOPK
    pool_view = cache.reshape(cache.shape[0], -1, cache.shape[3])

    def cache_spec(j, col0):
        def imap(b, g, pool_blk_ref, blk_ref):
            return (layer, pool_blk_ref[(b * G + g) * n_sel + j], col0 + g)
        return pl.BlockSpec((None, SEL_BLOCK, dh), imap)

    cache_specs = []
    for j in range(n_sel):
        cache_specs += [cache_spec(j, 2 * G), cache_spec(j, 3 * G)]
    tok = lambda col0: pl.BlockSpec((1, 1, dh), lambda b, g, *_: (b, 0, col0 + g))
    win = lambda col0: pl.BlockSpec((None, None, n_buf, dh), lambda b, g, *_: (layer, b, 0, col0 + g))
    grid_spec = pltpu.PrefetchScalarGridSpec(
        num_scalar_prefetch=2,
        grid=(B, G),
        in_specs=[pl.BlockSpec(memory_space=pltpu.SMEM),
                  pl.BlockSpec((1, 1, gs * dh), lambda b, g, *_: (b, 0, g)),
                  pl.BlockSpec((1, 1, gs, 3), lambda b, g, *_: (b, g, 0, 0)),
                  pl.BlockSpec((1, 1, gs, dh), lambda b, g, *_: (b, g, 0, 0)),
                  tok(2 * G), tok(3 * G), tok(0), tok(G), win(0), win(G)] + cache_specs,
        out_specs=pl.BlockSpec((1, 1, gs * dh), lambda b, g, *_: (b, 0, g)))
    return pl.pallas_call(
        functools.partial(_nsa_attend_kernel, n_groups=G, gs=gs, pos=pos),
        grid_spec=grid_spec,
        out_shape=jax.ShapeDtypeStruct((B, 1, n_heads * dh), BF16),
        compiler_params=_params("parallel", "arbitrary"),
        name="nsa_decode_attend",
    )(pool_blk.reshape(-1), blk.reshape(-1), _alibi_slopes(n_heads), q, gates, o_cmp,
      nsa_new, nsa_new, win_new, win_new, win_state, win_state, *([pool_view] * (2 * n_sel)))


def _trunk(x3, start, past, ln_g, ln_b, ffn_gate, ffn_up, ffn_down, attn_w_in, attn_w_out,
           cmp_pe, cmp_w1, cmp_w2, pool_w, pool_scale):
    B, T, D = x3.shape
    M = B * T
    depth = ffn_gate.shape[0]
    d_ff = ffn_gate.shape[-1]
    alpha = (2.0 * depth) ** 0.25
    decode = past is not None
    dh = HEAD_DIM
    n_heads = D // dh
    H_A = n_heads // 2
    NH = n_heads - H_A
    G = cmp_pe.shape[1]
    gs = NH // G
    moba_w, nsa_w, kvw = H_A * dh, NH * dh, G * dh
    tm = min(M, 1024)
    tm_ln = min(M, 256)
    pool_w_mx = pool_w.astype(MXU_DTYPE)
    if decode:
        assert T == 1
        cache_moba_kv, cache_nsa_kv, state_nsa_win, state_pool, page_table = past
        cache_moba = cache_moba_kv.reshape(cache_moba_kv.shape[:3] + (2 * moba_w,))
        cache_nsa = cache_nsa_kv.reshape(cache_nsa_kv.shape[:3] + (4 * kvw,))
        win_state_all = state_nsa_win.reshape(state_nsa_win.shape[:3] + (2 * kvw,))

    x = x3.reshape(M, D)
    xb = x.astype(BF16)
    new_moba, new_nsa, new_win, new_pool = [], [], [], []

    def ffn(x, xb, layer, sub, ln_idx):
        hidden = _ffn_up(xb, ffn_gate, ffn_up, layer, sub, tm)
        y = _matmul(hidden, ffn_down, (layer, sub), 0, D, F32, tm=tm, tn=256, kchunk=d_ff // 2, name="ffn_down")
        return _res_ln(y, x, ln_g[layer, ln_idx], ln_b[layer, ln_idx], alpha=alpha, coef=0.5, tm=tm_ln)

    for layer in range(depth):
        x, xb = ffn(x, xb, layer, 0, 0)
        if layer % 2 == 0:
            a = layer // 2
            proj = functools.partial(_matmul, xb, attn_w_in, (a,), tm=tm, name="in_proj")
            q_a = proj(0, moba_w, BF16, tn=256)
            moba_kv = proj(moba_w, 2 * moba_w, F32, tn=256)
            q_b = proj(3 * moba_w, nsa_w, BF16, tn=256)
            c0 = 3 * moba_w + nsa_w
            nsa_kv = proj(c0, 4 * kvw, F32, tn=256)
            win_kv = proj(c0 + 4 * kvw, 2 * kvw, F32, tn=256)
            gates = proj(c0 + 6 * kvw, 3 * NH, F32, tn=128)
            moba_new = moba_kv.reshape(B, T, 2, H_A, dh)
            nsa_new = nsa_kv.reshape(B, T, 4, G, dh)
            win_new = win_kv.reshape(B, T, 2, G, dh)
            q_a3, q_b3 = q_a.reshape(B, T, moba_w), q_b.reshape(B, T, nsa_w)
            moba_kv3, nsa_kv3, win_kv3 = (moba_kv.reshape(B, T, 2 * moba_w), nsa_kv.reshape(B, T, 4 * kvw),
                                          win_kv.reshape(B, T, 2 * kvw))
            if not decode:
                o_a = _moba_prompt(q_a3, moba_kv3, H_A)
                kv16 = nsa_new[:, :, 0:2].transpose(0, 2, 3, 1, 4).reshape(B, 2, G, T // 16, 16 * dh)
                cmp_kv = _compress_prompt(kv16, cmp_pe[a].reshape(2, 2, 16 * dh), cmp_w1, cmp_w2, a)
                gates_g = gates.reshape(B, T, 3, G, gs).transpose(0, 3, 1, 2, 4).reshape(B, G, T, 3 * gs)
                o_b = _nsa_prompt(q_b3, gates_g, cmp_kv, nsa_kv3, win_kv3, NH, G)
                win_state = win_new[:, -min(WINDOW, T):]
            else:
                top_a = _moba_decode_select(q_a3, cache_moba, a, page_table, H_A)
                o_a = _moba_decode_attend(q_a3, moba_kv3, cache_moba, a, page_table, top_a, H_A, start)
                cmp_kv = _nsa_decode_compress(cache_nsa, a, page_table, cmp_pe, cmp_w1, cmp_w2, G)
                o_cmp, top_b = _nsa_decode_select(q_b3, cmp_kv, NH, G, start)
                gates_g = gates.reshape(B, 3, G, gs).transpose(0, 2, 3, 1)
                o_b = _nsa_decode_attend(q_b3, gates_g, o_cmp, nsa_kv3, win_kv3, cache_nsa, win_state_all, a,
                                         page_table, top_b, NH, G, start)
                win_state = jnp.concatenate([state_nsa_win[a][:, T:], win_new], axis=1)
            mixed = jnp.concatenate([o_a, o_b], axis=-1).reshape(M, D)
            new_moba.append(moba_new)
            new_nsa.append(nsa_new)
            new_win.append(win_state)
            y = _matmul(mixed, attn_w_out, (a,), 0, D, F32, tm=tm, tn=256, name="out_proj")
            x, xb = _res_ln(y, x, ln_g[layer, 1], ln_b[layer, 1], alpha=alpha, coef=1.0, tm=tm_ln)
        else:
            p = layer // 2
            x3c = x.reshape(B, T, D)
            if decode:
                prev = jnp.concatenate([state_pool[p], x3c], axis=1)
                new_pool.append(prev[:, -POOL_BUF:])
                tt = T
            else:
                prev = x3c
                new_pool.append(x3c[:, -POOL_BUF:])
                tt = 256
            xo, xbo = _pool_mixer_ln(x3c, prev, pool_w_mx[p], pool_scale[p], ln_g[layer, 1], ln_b[layer, 1],
                                     alpha=alpha, start=start, tt=tt, decode=decode)
            x, xb = xo.reshape(M, D), xbo.reshape(M, D)
        x, xb = ffn(x, xb, layer, 1, 2)
    return (x.reshape(B, T, D), jnp.stack(new_moba), jnp.stack(new_nsa), jnp.stack(new_win),
            jnp.stack(new_pool))


def kernel(x_prompt, x_sample, cache_moba_kv, cache_nsa_kv, state_nsa_win, state_pool, page_table,
           ln_g, ln_b, ffn_gate, ffn_up, ffn_down, attn_w_in, attn_w_out, cmp_pe, cmp_w1, cmp_w2,
           pool_w, pool_scale):
    weights = (ln_g, ln_b, ffn_gate, ffn_up, ffn_down, attn_w_in, attn_w_out, cmp_pe, cmp_w1, cmp_w2,
               pool_w, pool_scale)
    past_len = page_table.shape[1] * cache_moba_kv.shape[2]
    y_p, moba_p, nsa_p, win_p, pool_p = _trunk(x_prompt, 0, None, *weights)
    y_s, moba_s, nsa_s, win_s, pool_s = _trunk(
        x_sample, past_len, (cache_moba_kv, cache_nsa_kv, state_nsa_win, state_pool, page_table), *weights)
    return (y_p, y_s, moba_p, nsa_p, win_p, pool_p, moba_s, nsa_s, win_s, pool_s)
```

```python
import functools

import jax
import jax.numpy as jnp
import numpy as np
from jax import lax
from jax.experimental import pallas as pl
from jax.experimental.pallas import tpu as pltpu

F32 = jnp.float32
BF16 = jnp.bfloat16
MXU_DTYPE = BF16

HEAD_DIM = 128
MOBA_BLOCK = 256
MOBA_TOPK = 3
CMP_LEN = 32
CMP_STRIDE = 16
SEL_BLOCK = 64
SEL_TOPK = 15
WINDOW = 512
POOL_WINDOWS = (2, 4, 8, 16)
POOL_BUF = max(POOL_WINDOWS) - 1
LN_EPS = 1e-5
SCALE = HEAD_DIM ** -0.5
NEG = -1e30
LOWEST = -3e38
NSA_Q_TILE = 128
NSA_KEY_CHUNK = 512
SUBLANES = 8
DECODE_PAGES_PER_STEP = 4

V7X_VMEM_BYTES = 64 * 1024 * 1024
VMEM_LIMIT = V7X_VMEM_BYTES - 8 * 1024 * 1024


def _params(*sem):
    return pltpu.CompilerParams(dimension_semantics=sem, vmem_limit_bytes=VMEM_LIMIT)


def _mx(a):
    return a.astype(MXU_DTYPE)


def _dot(a, b):
    return jnp.dot(_mx(a), _mx(b), preferred_element_type=F32)


def _dot_nt(a, b):
    return lax.dot_general(_mx(a), _mx(b), (((1,), (1,)), ((), ())), preferred_element_type=F32)


def _alibi_slopes(n):
    return jnp.asarray(np.exp2(-8.0 * np.arange(1, n + 1) / n), dtype=F32)


def _layernorm_rows(z, g, b):
    mu = jnp.mean(z, axis=-1, keepdims=True)
    zc = z - mu
    var = jnp.mean(zc * zc, axis=-1, keepdims=True)
    return zc * lax.rsqrt(var + LN_EPS) * g + b


def _ffn_up_kernel(x_ref, wg_ref, wu_ref, o_ref):
    x = x_ref[...]
    g = _dot(x, wg_ref[...])
    u = _dot(x, wu_ref[...])
    o_ref[...] = (g * jax.nn.sigmoid(g) * u).astype(o_ref.dtype)


def _ffn_up(xb, w_gate, w_up, layer, sub, tm, tn=256):
    M, D = xb.shape
    F = w_gate.shape[-1]
    assert M % tm == 0 and F % tn == 0
    wspec = pl.BlockSpec((None, None, D, tn), lambda i, j: (layer, sub, 0, j))
    return pl.pallas_call(
        _ffn_up_kernel,
        grid=(M // tm, F // tn),
        in_specs=[pl.BlockSpec((tm, D), lambda i, j: (i, 0)), wspec, wspec],
        out_specs=pl.BlockSpec((tm, tn), lambda i, j: (i, j)),
        out_shape=jax.ShapeDtypeStruct((M, F), BF16),
        compiler_params=_params("parallel", "arbitrary"),
        name="ffn_up",
    )(xb, w_gate, w_up)


def _matmul_kernel(x_ref, w_ref, o_ref, *, kchunk):
    K = x_ref.shape[1]
    acc = None
    for k0 in range(0, K, kchunk):
        part = _dot(x_ref[:, k0:k0 + kchunk], w_ref[k0:k0 + kchunk, :])
        acc = part if acc is None else acc + part
    o_ref[...] = acc[:, :o_ref.shape[1]].astype(o_ref.dtype)


def _matmul(xb, w, widx, col0, ncols, out_dtype, *, tm, tn, kchunk=None, name):
    M, K = xb.shape
    assert M % tm == 0 and col0 % tn == 0
    kchunk = K if kchunk is None else kchunk
    assert K % kchunk == 0
    if ncols % tn == 0:
        nj, out_tn = ncols // tn, tn
    else:
        assert ncols < tn
        nj, out_tn = 1, ncols
    j0 = col0 // tn
    lead = (None,) * len(widx)
    x_spec = pl.BlockSpec((tm, K), lambda i, j: (i, 0), pipeline_mode=pl.Buffered(1))
    return pl.pallas_call(
        functools.partial(_matmul_kernel, kchunk=kchunk),
        grid=(M // tm, nj),
        in_specs=[x_spec, pl.BlockSpec(lead + (K, tn), lambda i, j: tuple(widx) + (0, j0 + j))],
        out_specs=pl.BlockSpec((tm, out_tn), lambda i, j: (i, j)),
        out_shape=jax.ShapeDtypeStruct((M, ncols), out_dtype),
        compiler_params=_params("parallel", "arbitrary"),
        name=name,
    )(xb, w)


def _res_ln_kernel(y_ref, res_ref, g_ref, b_ref, o_ref, ob_ref, *, alpha, coef):
    z = alpha * res_ref[...] + coef * y_ref[...]
    y = _layernorm_rows(z, g_ref[...], b_ref[...])
    o_ref[...] = y
    ob_ref[...] = y.astype(ob_ref.dtype)


def _res_ln(y, res, g, b, *, alpha, coef, tm):
    M, D = y.shape
    assert M % tm == 0
    row = pl.BlockSpec((1, D), lambda i: (0, 0))
    tile = pl.BlockSpec((tm, D), lambda i: (i, 0))
    return pl.pallas_call(
        functools.partial(_res_ln_kernel, alpha=alpha, coef=coef),
        grid=(M // tm,),
        in_specs=[tile, tile, row, row],
        out_specs=[tile, tile],
        out_shape=[jax.ShapeDtypeStruct((M, D), F32), jax.ShapeDtypeStruct((M, D), BF16)],
        compiler_params=_params("parallel"),
        name="res_ln",
    )(y, res, g.reshape(1, D), b.reshape(1, D))


def _pool_kernel(x_ref, prev_ref, w_ref, sc_ref, g_ref, b_ref, o_ref, ob_ref, *,
                 alpha, start, tt, decode, group):
    ti = pl.program_id(1)
    zs = []
    for gi, win in enumerate(POOL_WINDOWS):
        c0, c1 = gi * group, (gi + 1) * group
        prev = prev_ref[0, :, c0:c1]
        if decode:
            xg = prev[POOL_BUF:POOL_BUF + 1]
            wsum = jnp.sum(prev[POOL_BUF + 1 - win:], axis=0, keepdims=True)
            cnt = jnp.full((1, 1), float(min(win, start + 1)), F32)
        else:
            xg = x_ref[0, :, c0:c1]
            prev = jnp.where(ti == 0, 0.0, prev)
            s = jnp.concatenate([prev, xg], axis=0)
            span = 1
            while span < win:
                s = s + pltpu.roll(s, span, axis=0)
                span *= 2
            wsum = s[POOL_BUF + 1:]
            pos = start + ti * tt + lax.broadcasted_iota(jnp.int32, (tt, 1), 0)
            cnt = jnp.minimum(win, pos + 1).astype(F32)
        pooled = wsum / cnt - xg
        y = _dot(pooled, w_ref[gi]) * sc_ref[:, c0:c1]
        zs.append(alpha * xg + y)
    d = group * len(POOL_WINDOWS)
    mu = sum(jnp.sum(z, axis=-1, keepdims=True) for z in zs) / d
    var = sum(jnp.sum((z - mu) * (z - mu), axis=-1, keepdims=True) for z in zs) / d
    inv = lax.rsqrt(var + LN_EPS)
    for gi, z in enumerate(zs):
        c0, c1 = gi * group, (gi + 1) * group
        y = (z - mu) * inv * g_ref[:, c0:c1] + b_ref[:, c0:c1]
        o_ref[0, :, c0:c1] = y
        ob_ref[0, :, c0:c1] = y.astype(ob_ref.dtype)


def _pool_mixer_ln(x3, prev3, w_pool, scale, g, b, *, alpha, start, tt, decode):
    B, T, D = x3.shape
    group = D // len(POOL_WINDOWS)
    assert T % tt == 0 and (decode or tt % 16 == 0)
    per = tt // 16 if not decode else 0
    if decode:
        prev_map = lambda bi, ti: (bi, 0, 0)
    else:
        prev_map = lambda bi, ti: (bi, jnp.maximum(ti * per - 1, 0), 0)
    row = pl.BlockSpec((1, D), lambda bi, ti: (0, 0))
    tile = pl.BlockSpec((1, tt, D), lambda bi, ti: (bi, ti, 0))
    return pl.pallas_call(
        functools.partial(_pool_kernel, alpha=alpha, start=start, tt=tt, decode=decode, group=group),
        grid=(B, T // tt),
        in_specs=[tile, pl.BlockSpec((1, 16, D), prev_map),
                  pl.BlockSpec(w_pool.shape, lambda bi, ti: (0, 0, 0)), row, row, row],
        out_specs=[tile, tile],
        out_shape=[jax.ShapeDtypeStruct((B, T, D), F32), jax.ShapeDtypeStruct((B, T, D), BF16)],
        compiler_params=_params("parallel", "arbitrary"),
        name="pool_mixer_ln",
    )(x3, prev3, w_pool, scale.reshape(1, D), g.reshape(1, D), b.reshape(1, D))


def _topk_mask(score, col, n_valid, k):
    n = score.shape[1]
    masked = jnp.where(col < n_valid, score, NEG)
    rank = jnp.zeros(score.shape, jnp.int32)
    for j in range(n):
        other = masked[:, j:j + 1]
        ahead = (other > masked) | ((other == masked) & (j < col))
        rank = rank + ahead.astype(jnp.int32)
    return (rank < k) & (col < n_valid)


def _online_update(m_ref, l_ref, acc_ref, s, v):
    m_old = m_ref[...]
    m_new = jnp.maximum(m_old, jnp.max(s, axis=-1, keepdims=True))
    p = jnp.exp(s - m_new)
    a = jnp.exp(m_old - m_new)
    l_ref[...] = a * l_ref[...] + jnp.sum(p, axis=-1, keepdims=True)
    acc_ref[...] = a * acc_ref[...] + _dot(p, v)
    m_ref[...] = m_new


def _moba_kernel(slopes_ref, q_ref, k_ref, v_ref, o_ref, m_ref, l_ref, acc_ref, sel_ref, *, nb):
    h = pl.program_id(1)
    qi = pl.program_id(2)
    blk = MOBA_BLOCK
    slope = slopes_ref[h]
    q = q_ref[0]
    kmean = jnp.concatenate(
        [jnp.sum(k_ref[0, n * blk:(n + 1) * blk, :], axis=0, keepdims=True) for n in range(nb)],
        axis=0) * (1.0 / blk)
    gate = _dot_nt(q, kmean)
    col = lax.broadcasted_iota(jnp.int32, gate.shape, 1)
    sel_ref[...] = _topk_mask(gate, col, qi, MOBA_TOPK).astype(F32)

    rel = lax.broadcasted_iota(jnp.int32, (blk, blk), 0) - lax.broadcasted_iota(jnp.int32, (blk, blk), 1)
    alibi = slope * rel.astype(F32)

    own = pl.multiple_of(qi * blk, blk)
    s = _dot_nt(q, k_ref[0, pl.ds(own, blk), :]) * SCALE - alibi + jnp.where(rel >= 0, 0.0, NEG)
    m = jnp.max(s, axis=-1, keepdims=True)
    p = jnp.exp(s - m)
    m_ref[...] = m
    l_ref[...] = jnp.sum(p, axis=-1, keepdims=True)
    acc_ref[...] = _dot(p, v_ref[0, pl.ds(own, blk), :])

    def block(n, carry):
        start = pl.multiple_of(n * blk, blk)
        picked = jnp.sum(jnp.where(col == n, sel_ref[...], 0.0), axis=-1, keepdims=True) > 0.5
        row_term = jnp.where(picked, 0.0, NEG) - slope * ((qi - n) * blk).astype(F32)
        s = _dot_nt(q, k_ref[0, pl.ds(start, blk), :]) * SCALE - alibi + row_term
        _online_update(m_ref, l_ref, acc_ref, s, v_ref[0, pl.ds(start, blk), :])
        return carry

    lax.fori_loop(0, qi, block, 0)
    o_ref[0] = (acc_ref[...] / l_ref[...]).astype(o_ref.dtype)


def _moba_prompt(q, kv, n_heads):
    B, T, _ = q.shape
    assert T % MOBA_BLOCK == 0
    nb = T // MOBA_BLOCK
    dh = HEAD_DIM
    return pl.pallas_call(
        functools.partial(_moba_kernel, nb=nb),
        grid=(B, n_heads, nb),
        in_specs=[pl.BlockSpec(memory_space=pltpu.SMEM),
                  pl.BlockSpec((1, MOBA_BLOCK, dh), lambda b, h, i: (b, i, h)),
                  pl.BlockSpec((1, T, dh), lambda b, h, i: (b, 0, h)),
                  pl.BlockSpec((1, T, dh), lambda b, h, i: (b, 0, n_heads + h))],
        out_specs=pl.BlockSpec((1, MOBA_BLOCK, dh), lambda b, h, i: (b, i, h)),
        out_shape=jax.ShapeDtypeStruct((B, T, n_heads * dh), BF16),
        scratch_shapes=[pltpu.VMEM((MOBA_BLOCK, 1), F32), pltpu.VMEM((MOBA_BLOCK, 1), F32),
                        pltpu.VMEM((MOBA_BLOCK, dh), F32), pltpu.VMEM((MOBA_BLOCK, nb), F32)],
        compiler_params=_params("parallel", "parallel", "arbitrary"),
        name="moba_prompt",
    )(_alibi_slopes(n_heads), q, kv, kv)


def _compress_kernel(x_ref, pe_ref, w1_ref, w2_ref, o_ref, *, nc):
    x = x_ref[0, 0, 0]
    half = x.shape[1]
    lo = _dot(x + pe_ref[0, 0:1, :], w1_ref[:half, :])
    hi = _dot(x + pe_ref[0, 1:2, :], w1_ref[half:, :])
    rows = x.shape[0]
    hid = lo + pltpu.roll(hi, rows - 1, axis=0)
    y = _dot(hid * jax.nn.sigmoid(hid), w2_ref[...])
    ridx = lax.broadcasted_iota(jnp.int32, y.shape, 0)
    o_ref[0, 0, 0] = jnp.where(ridx < nc, y, 0.0)


def _compress_prompt(kv16, pe, w1, w2, layer):
    B, two, G, R, W = kv16.shape
    nc = R - 1
    hidden = w1.shape[-1]
    return pl.pallas_call(
        functools.partial(_compress_kernel, nc=nc),
        grid=(B, two, G),
        in_specs=[pl.BlockSpec((1, 1, 1, R, W), lambda b, c, g: (b, c, g, 0, 0)),
                  pl.BlockSpec((1, 2, W), lambda b, c, g: (c, 0, 0)),
                  pl.BlockSpec((None, None, 2 * W, hidden), lambda b, c, g: (layer, c, 0, 0)),
                  pl.BlockSpec((None, None, hidden, HEAD_DIM), lambda b, c, g: (layer, c, 0, 0))],
        out_specs=pl.BlockSpec((1, 1, 1, R, HEAD_DIM), lambda b, c, g: (b, c, g, 0, 0)),
        out_shape=jax.ShapeDtypeStruct((B, two, G, R, HEAD_DIM), F32),
        compiler_params=_params("parallel", "parallel", "parallel"),
        name="nsa_compress",
    )(kv16, pe, w1, w2)


def _nsa_kernel(slopes_ref, q_ref, gates_ref, kc_ref, vc_ref, ks_ref, vs_ref, kw_ref, vw_ref,
                ov_ref, ex_ref, o_ref, m_ref, l_ref, acc_ref, *, T, gs):
    g = pl.program_id(1)
    qi = pl.program_id(2)
    tq = NSA_Q_TILE
    dh = HEAD_DIM
    q = q_ref[0]
    qs = jnp.concatenate([q[:, s * dh:(s + 1) * dh] for s in range(gs)], axis=0)
    slopes = [slopes_ref[g * gs + s] for s in range(gs)]
    heads = [(s * tq, (s + 1) * tq) for s in range(gs)]
    tpos = qi * tq + lax.broadcasted_iota(jnp.int32, (tq, 1), 0)

    ncp = kc_ref.shape[3]
    c_end = lax.broadcasted_iota(jnp.int32, (1, ncp), 1) * CMP_STRIDE + (CMP_LEN - 1)
    dist_c = tpos - c_end
    cmask = dist_c >= 0
    bias_c = jnp.where(cmask, 0.0, NEG)
    distf_c = dist_c.astype(F32)
    s_all = _dot_nt(qs, kc_ref[0, 0, 0])
    p_list = []
    for s, (r0, r1) in enumerate(heads):
        sc = s_all[r0:r1] * SCALE - slopes[s] * distf_c + bias_c
        e = jnp.where(cmask, jnp.exp(sc - jnp.max(sc, axis=-1, keepdims=True)), 0.0)
        den = jnp.sum(e, axis=-1, keepdims=True)
        p_list.append(e / jnp.where(den > 0.0, den, 1.0))
    p_c = _mx(jnp.concatenate(p_list, axis=0))
    o_cmp = _dot(p_c, vc_ref[0, 0, 0])

    imp_rows = _dot(p_c, ov_ref[...])
    imp = imp_rows[0:tq]
    for r0, r1 in heads[1:]:
        imp = imp + imp_rows[r0:r1]
    ns = imp.shape[1]
    colj = lax.broadcasted_iota(jnp.int32, (tq, ns), 1)
    sel = _topk_mask(imp, colj, tpos // SEL_BLOCK, SEL_TOPK).astype(F32)
    sel_keys = _dot(sel, ex_ref[...])

    m_ref[...] = jnp.full(m_ref.shape, NEG, F32)
    l_ref[...] = jnp.zeros(l_ref.shape, F32)
    acc_ref[...] = jnp.zeros(acc_ref.shape, F32)
    kc = NSA_KEY_CHUNK
    for c in reversed(range(T // kc)):
        @pl.when(c * kc < (qi + 1) * tq)
        def _():
            kpos = c * kc + lax.broadcasted_iota(jnp.int32, (1, kc), 1)
            dist = tpos - kpos
            own = (kpos // SEL_BLOCK) == (tpos // SEL_BLOCK)
            ok = ((sel_keys[:, c * kc:(c + 1) * kc] > 0.5) | own) & (dist >= 0)
            bias = jnp.where(ok, 0.0, NEG)
            distf = dist.astype(F32)
            s_all = _dot_nt(qs, ks_ref[0, c * kc:(c + 1) * kc, :])
            m_all = m_ref[...]
            ps, maxes, sums = [], [], []
            for s, (r0, r1) in enumerate(heads):
                sc = s_all[r0:r1] * SCALE - slopes[s] * distf + bias
                m_new = jnp.maximum(m_all[r0:r1], jnp.max(sc, axis=-1, keepdims=True))
                p = jnp.exp(sc - m_new)
                maxes.append(m_new)
                sums.append(jnp.sum(p, axis=-1, keepdims=True))
                ps.append(_mx(p))
            m_new = jnp.concatenate(maxes, axis=0)
            a = jnp.exp(m_all - m_new)
            pv = _dot(jnp.concatenate(ps, axis=0), vs_ref[0, c * kc:(c + 1) * kc, :])
            l_ref[...] = a * l_ref[...] + jnp.concatenate(sums, axis=0)
            acc_ref[...] = a * acc_ref[...] + pv
            m_ref[...] = m_new
    o_sel = acc_ref[...] / l_ref[...]

    span = WINDOW + tq
    w0 = pl.multiple_of(jnp.maximum(qi * tq - WINDOW, 0), tq)
    kpos = w0 + lax.broadcasted_iota(jnp.int32, (1, span), 1)
    dist = tpos - kpos
    bias = jnp.where((dist >= 0) & (dist < WINDOW), 0.0, NEG)
    distf = dist.astype(F32)
    s_all = _dot_nt(qs, kw_ref[0, pl.ds(w0, span), :])
    ps = []
    for s, (r0, r1) in enumerate(heads):
        sc = s_all[r0:r1] * SCALE - slopes[s] * distf + bias
        e = jnp.exp(sc - jnp.max(sc, axis=-1, keepdims=True))
        ps.append(_mx(e / jnp.sum(e, axis=-1, keepdims=True)))
    o_win = _dot(jnp.concatenate(ps, axis=0), vw_ref[0, pl.ds(w0, span), :])

    gate = jax.nn.sigmoid(gates_ref[0, 0])
    for s, (r0, r1) in enumerate(heads):
        o = (gate[:, s:s + 1] * o_cmp[r0:r1] + gate[:, gs + s:gs + s + 1] * o_sel[r0:r1]
             + gate[:, 2 * gs + s:2 * gs + s + 1] * o_win[r0:r1])
        o_ref[0, :, s * dh:(s + 1) * dh] = o.astype(o_ref.dtype)


def _nsa_tables(T, rows_cmp):
    ns = -(-T // SEL_BLOCK)
    c_start = np.arange(rows_cmp) * CMP_STRIDE
    s_start = np.arange(ns) * SEL_BLOCK
    nc = (T - CMP_LEN) // CMP_STRIDE + 1
    overlap = ((c_start[:, None] <= s_start[None, :] + SEL_BLOCK - 1)
               & (c_start[:, None] + CMP_LEN - 1 >= s_start[None, :])
               & (np.arange(rows_cmp)[:, None] < nc))
    expand = (np.arange(T)[None, :] // SEL_BLOCK) == np.arange(ns)[:, None]
    return jnp.asarray(overlap.astype(np.float32), BF16), jnp.asarray(expand.astype(np.float32), BF16)


def _nsa_prompt(q, gates, cmp_kv, nsa_kv, win_kv, n_heads, n_groups):
    B, T, _ = q.shape
    G = n_groups
    gs = n_heads // G
    dh = HEAD_DIM
    tq = NSA_Q_TILE
    assert T % NSA_KEY_CHUNK == 0 and T >= WINDOW + tq and T % tq == 0
    R = cmp_kv.shape[3]
    overlap, expand = _nsa_tables(T, R)
    ns = overlap.shape[1]
    rows = gs * tq
    seq = lambda col: pl.BlockSpec((1, T, dh), lambda b, g, i: (b, 0, col(g)))
    return pl.pallas_call(
        functools.partial(_nsa_kernel, T=T, gs=gs),
        grid=(B, G, T // tq),
        in_specs=[pl.BlockSpec(memory_space=pltpu.SMEM),
                  pl.BlockSpec((1, tq, gs * dh), lambda b, g, i: (b, i, g)),
                  pl.BlockSpec((1, 1, tq, 3 * gs), lambda b, g, i: (b, g, i, 0)),
                  pl.BlockSpec((1, 1, 1, R, dh), lambda b, g, i: (b, 0, g, 0, 0)),
                  pl.BlockSpec((1, 1, 1, R, dh), lambda b, g, i: (b, 1, g, 0, 0)),
                  seq(lambda g: 2 * G + g), seq(lambda g: 3 * G + g),
                  seq(lambda g: g), seq(lambda g: G + g),
                  pl.BlockSpec((R, ns), lambda b, g, i: (0, 0)),
                  pl.BlockSpec((ns, T), lambda b, g, i: (0, 0))],
        out_specs=pl.BlockSpec((1, tq, gs * dh), lambda b, g, i: (b, i, g)),
        out_shape=jax.ShapeDtypeStruct((B, T, n_heads * dh), BF16),
        scratch_shapes=[pltpu.VMEM((rows, 1), F32), pltpu.VMEM((rows, 1), F32), pltpu.VMEM((rows, dh), F32)],
        compiler_params=_params("parallel", "parallel", "arbitrary"),
        name="nsa_prompt",
    )(_alibi_slopes(n_heads), q, gates, cmp_kv, cmp_kv, nsa_kv, nsa_kv, win_kv, win_kv, overlap, expand)


def _first_argmax(score, lane):
    mx = jnp.max(score, axis=-1, keepdims=True)
    return jnp.min(jnp.where(score == mx, lane, float(score.shape[1])), axis=-1, keepdims=True)


def _topk_indices(score, k, width):
    lane = lax.broadcasted_iota(jnp.int32, score.shape, 1).astype(F32)
    out_lane = lax.broadcasted_iota(jnp.int32, (score.shape[0], width), 1)
    out = jnp.zeros((score.shape[0], width), jnp.int32)
    for j in range(k):
        idx = _first_argmax(score, lane)
        out = jnp.where(out_lane == j, idx.astype(jnp.int32), out)
        score = jnp.where(lane == idx, LOWEST, score)
    return out


def _moba_select_kernel(pt_ref, q_ref, *rest, n_heads, n_steps, pages_per_block):
    cache_refs, (idx_ref, ksum_ref) = rest[:-2], rest[-2:]
    st = pl.program_id(1)
    blocks_per_step = len(cache_refs) // pages_per_block
    for i in range(blocks_per_step):
        total = None
        for r in range(pages_per_block):
            part = jnp.sum(cache_refs[i * pages_per_block + r][...], axis=0, keepdims=True)
            total = part if total is None else total + part
        ksum_ref[pl.ds(st * blocks_per_step + i, 1)] = total

    @pl.when(st == n_steps - 1)
    def _():
        dh = HEAD_DIM
        q = q_ref[0]
        gate = jnp.concatenate(
            [_dot_nt(q[:, h * dh:(h + 1) * dh], ksum_ref[:, h, :] * (1.0 / MOBA_BLOCK)) for h in range(n_heads)],
            axis=0)
        idx_ref[0] = _topk_indices(gate, MOBA_TOPK, idx_ref.shape[2])


def _moba_decode_select(q, cache, layer, page_table, n_heads):
    B, n_pages = page_table.shape
    page = cache.shape[2]
    width = n_heads * HEAD_DIM
    ppb = MOBA_BLOCK // page
    pps = DECODE_PAGES_PER_STEP
    assert MOBA_BLOCK % page == 0 and pps % ppb == 0 and n_pages % pps == 0 and n_pages // ppb >= MOBA_TOPK
    n_steps = n_pages // pps

    def cache_spec(i):
        return pl.BlockSpec((None, None, page, None, n_heads, HEAD_DIM),
                            lambda b, st, pt: (layer, pt[b * n_pages + st * pps + i], 0, 0, 0, 0))

    grid_spec = pltpu.PrefetchScalarGridSpec(
        num_scalar_prefetch=1,
        grid=(B, n_steps),
        in_specs=[pl.BlockSpec((1, 1, width), lambda b, st, pt: (b, 0, 0))] + [cache_spec(i) for i in range(pps)],
        out_specs=pl.BlockSpec((1, n_heads, 128), lambda b, st, pt: (b, 0, 0)),
        scratch_shapes=[pltpu.VMEM((n_pages // ppb, n_heads, HEAD_DIM), F32)])
    return pl.pallas_call(
        functools.partial(_moba_select_kernel, n_heads=n_heads, n_steps=n_steps, pages_per_block=ppb),
        grid_spec=grid_spec,
        out_shape=jax.ShapeDtypeStruct((B, n_heads, 128), jnp.int32),
        compiler_params=_params("parallel", "arbitrary"),
        name="moba_decode_select",
    )(page_table.reshape(-1), q, *([cache] * pps))


def _moba_attend_kernel(phys_ref, blk_ref, slopes_ref, q_ref, kn_ref, vn_ref, *rest, n_heads, pos, page, ppb):
    cache_refs, o_ref = rest[:-1], rest[-1]
    b = pl.program_id(0)
    h = pl.program_id(1)
    slope = slopes_ref[h]
    q = q_ref[0]
    n_sel = len(cache_refs) // (2 * ppb)
    rows = page * SUBLANES
    ridx = lax.broadcasted_iota(jnp.int32, (1, rows), 1)
    mine = (ridx % SUBLANES) == (h % SUBLANES)
    scores, values = [], []
    for j in range(n_sel):
        blk = blk_ref[(b * n_heads + h) * n_sel + j]
        for r in range(ppb):
            k_ref, v_ref = cache_refs[2 * (j * ppb + r)], cache_refs[2 * (j * ppb + r) + 1]
            dist = pos - (blk * MOBA_BLOCK + r * page + ridx // SUBLANES)
            s = _dot_nt(q, k_ref[...].reshape(rows, HEAD_DIM)) * SCALE - slope * dist.astype(F32)
            scores.append(jnp.where(mine, s, NEG))
            values.append(v_ref)
    s_new = jnp.sum(_mx(q).astype(F32) * _mx(kn_ref[0]).astype(F32), axis=-1, keepdims=True) * SCALE
    m = s_new
    for s in scores:
        m = jnp.maximum(m, jnp.max(s, axis=-1, keepdims=True))
    p_new = jnp.exp(s_new - m)
    den = p_new
    acc = _mx(p_new).astype(F32) * _mx(vn_ref[0]).astype(F32)
    for s, v_ref in zip(scores, values):
        p = jnp.exp(s - m)
        den = den + jnp.sum(p, axis=-1, keepdims=True)
        acc = acc + _dot(p, v_ref[...].reshape(rows, HEAD_DIM))
    o_ref[0] = (acc / den).astype(o_ref.dtype)


def _moba_decode_attend(q, kv_new, cache, layer, page_table, top_idx, n_heads, pos):
    B, n_pages = page_table.shape
    page = cache.shape[2]
    dh = HEAD_DIM
    ppb = MOBA_BLOCK // page
    blk = top_idx[:, :, :MOBA_TOPK]
    pages = blk[..., None] * ppb + jnp.arange(ppb, dtype=jnp.int32)
    phys = jnp.take_along_axis(page_table[:, None, :], pages.reshape(B, 1, -1), axis=2)
    n_sel = MOBA_TOPK

    assert n_heads % SUBLANES == 0
    hb = n_heads // SUBLANES
    cache = cache.reshape(cache.shape[:3] + (2 * hb, SUBLANES, dh))

    def cache_spec(j, r, kv):
        def imap(b, h, phys_ref, blk_ref):
            return (layer, phys_ref[((b * n_heads + h) * n_sel + j) * ppb + r], 0, kv * hb + h // SUBLANES, 0, 0)
        return pl.BlockSpec((None, None, page, None, SUBLANES, dh), imap)

    cache_specs = []
    for j in range(n_sel):
        for r in range(ppb):
            cache_specs += [cache_spec(j, r, 0), cache_spec(j, r, 1)]
    tok = lambda col0: pl.BlockSpec((1, 1, dh), lambda b, h, *_: (b, 0, col0 + h))
    grid_spec = pltpu.PrefetchScalarGridSpec(
        num_scalar_prefetch=2,
        grid=(B, n_heads),
        in_specs=[pl.BlockSpec(memory_space=pltpu.SMEM), tok(0), tok(0), tok(n_heads)] + cache_specs,
        out_specs=tok(0))
    return pl.pallas_call(
        functools.partial(_moba_attend_kernel, n_heads=n_heads, pos=pos, page=page, ppb=ppb),
        grid_spec=grid_spec,
        out_shape=jax.ShapeDtypeStruct((B, 1, n_heads * dh), BF16),
        compiler_params=_params("parallel", "arbitrary"),
        name="moba_decode_attend",
    )(phys.reshape(-1), blk.reshape(-1), _alibi_slopes(n_heads), q, kv_new, kv_new,
      *([cache] * (2 * n_sel * ppb)))


def _nsa_decode_compress_kernel(pt_ref, *refs, n_steps, page, n_groups):
    n_in = len(refs) - 5
    cache_refs = refs[:n_in]
    pe_ref, w1_ref, w2_ref, o_ref, buf_ref = refs[n_in:]
    st = pl.program_id(1)
    pps = n_in // 2
    for i in range(pps):
        row0 = pl.multiple_of((st * pps + i) * page, page)
        for c in range(2 * n_groups):
            buf_ref[c, pl.ds(row0, page), :] = cache_refs[2 * i + c // n_groups][:, c % n_groups, :]

    @pl.when(st == n_steps - 1)
    def _():
        dh = HEAD_DIM
        rows = buf_ref.shape[1] // CMP_STRIDE
        halves = CMP_LEN // CMP_STRIDE
        for c in range(2 * n_groups):
            which = c // n_groups
            parts = [None] * halves
            for l in range(CMP_STRIDE):
                x = buf_ref.at[c][pl.ds(l, rows, stride=CMP_STRIDE), :]
                for hf in range(halves):
                    ll = hf * CMP_STRIDE + l
                    d = _dot(x + pe_ref[which, ll:ll + 1, :], w1_ref[which, ll * dh:(ll + 1) * dh, :])
                    parts[hf] = d if parts[hf] is None else parts[hf] + d
            hid = parts[0]
            for hf in range(1, halves):
                hid = hid + pltpu.roll(parts[hf], rows - hf, axis=0)
            y = _dot(hid * jax.nn.sigmoid(hid), w2_ref[which])
            ridx = lax.broadcasted_iota(jnp.int32, y.shape, 0)
            o_ref[0, c] = jnp.where(ridx < rows - (halves - 1), y, 0.0)


def _nsa_decode_compress(cache, layer, page_table, pe, w1, w2, n_groups):
    B, n_pages = page_table.shape
    page = cache.shape[2]
    dh = HEAD_DIM
    past = n_pages * page
    rows = past // CMP_STRIDE
    hidden = w1.shape[-1]

    pps = DECODE_PAGES_PER_STEP
    assert n_pages % pps == 0
    n_steps = n_pages // pps

    def cache_spec(i, which):
        return pl.BlockSpec((None, None, page, None, n_groups, dh),
                            lambda b, st, pt: (layer, pt[b * n_pages + st * pps + i], 0, which, 0, 0))

    cache_specs = [cache_spec(i, which) for i in range(pps) for which in range(2)]
    grid_spec = pltpu.PrefetchScalarGridSpec(
        num_scalar_prefetch=1,
        grid=(B, n_steps),
        in_specs=cache_specs + [
                  pl.BlockSpec((None, 2, CMP_LEN, dh), lambda b, p, pt: (layer, 0, 0, 0)),
                  pl.BlockSpec((None, 2, CMP_LEN * dh, hidden), lambda b, p, pt: (layer, 0, 0, 0)),
                  pl.BlockSpec((None, 2, hidden, dh), lambda b, p, pt: (layer, 0, 0, 0))],
        out_specs=pl.BlockSpec((1, 2 * n_groups, rows, dh), lambda b, p, pt: (b, 0, 0, 0)),
        scratch_shapes=[pltpu.VMEM((2 * n_groups, past, dh), F32)])
    return pl.pallas_call(
        functools.partial(_nsa_decode_compress_kernel, n_steps=n_steps, page=page, n_groups=n_groups),
        grid_spec=grid_spec,
        out_shape=jax.ShapeDtypeStruct((B, 2 * n_groups, rows, dh), F32),
        compiler_params=_params("parallel", "arbitrary"),
        name="nsa_decode_compress",
    )(page_table.reshape(-1), *([cache] * (2 * pps)), pe, w1, w2)


def _split_heads(q, gs):
    dh = HEAD_DIM
    return jnp.concatenate([q[:, s * dh:(s + 1) * dh] for s in range(gs)], axis=0)


def _group_slopes(slopes_ref, g, gs):
    sub = lax.broadcasted_iota(jnp.int32, (gs, 1), 0)
    slope = jnp.zeros((gs, 1), F32)
    for s in range(gs):
        slope = jnp.where(sub == s, slopes_ref[g * gs + s], slope)
    return slope


def _nsa_select_kernel(slopes_ref, q_ref, kc_ref, vc_ref, ov_ref, ocmp_ref, idx_ref, *, gs, pos):
    g = pl.program_id(1)
    qs = _split_heads(q_ref[0], gs)
    slope = _group_slopes(slopes_ref, g, gs)
    ncp = kc_ref.shape[2]
    c_end = lax.broadcasted_iota(jnp.int32, (1, ncp), 1) * CMP_STRIDE + (CMP_LEN - 1)
    dist_c = pos - c_end
    cmask = dist_c >= 0
    s_c = jnp.where(cmask, _dot_nt(qs, kc_ref[0, 0]) * SCALE - slope * dist_c.astype(F32), NEG)
    e_c = jnp.where(cmask, jnp.exp(s_c - jnp.max(s_c, axis=-1, keepdims=True)), 0.0)
    den = jnp.sum(e_c, axis=-1, keepdims=True)
    p_c = e_c / jnp.where(den > 0.0, den, 1.0)
    ocmp_ref[0, 0] = _dot(p_c, vc_ref[0, 0])
    imp = jnp.sum(_dot(p_c, ov_ref[...]), axis=0, keepdims=True)
    idx = _topk_indices(imp, SEL_TOPK, idx_ref.shape[3])
    idx_ref[0, 0] = jnp.broadcast_to(idx, idx_ref.shape[2:])


def _nsa_decode_select(q, cmp_kv, n_heads, n_groups, pos):
    B = q.shape[0]
    G = n_groups
    gs = n_heads // G
    dh = HEAD_DIM
    R = cmp_kv.shape[2]
    cur = pos // SEL_BLOCK
    assert pos % SEL_BLOCK == 0 and cur >= SEL_TOPK
    c_start = np.arange(R) * CMP_STRIDE
    s_start = np.arange(cur) * SEL_BLOCK
    nc = (pos + 1 - CMP_LEN) // CMP_STRIDE + 1
    overlap = ((c_start[:, None] <= s_start[None, :] + SEL_BLOCK - 1)
               & (c_start[:, None] + CMP_LEN - 1 >= s_start[None, :])
               & (np.arange(R)[:, None] < nc))
    overlap = jnp.asarray(overlap.astype(np.float32), BF16)
    return pl.pallas_call(
        functools.partial(_nsa_select_kernel, gs=gs, pos=pos),
        grid=(B, G),
        in_specs=[pl.BlockSpec(memory_space=pltpu.SMEM),
                  pl.BlockSpec((1, 1, gs * dh), lambda b, g: (b, 0, g)),
                  pl.BlockSpec((1, 1, R, dh), lambda b, g: (b, g, 0, 0)),
                  pl.BlockSpec((1, 1, R, dh), lambda b, g: (b, G + g, 0, 0)),
                  pl.BlockSpec((R, cur), lambda b, g: (0, 0))],
        out_specs=[pl.BlockSpec((1, 1, gs, dh), lambda b, g: (b, g, 0, 0)),
                   pl.BlockSpec((1, 1, 8, 128), lambda b, g: (b, g, 0, 0))],
        out_shape=[jax.ShapeDtypeStruct((B, G, gs, dh), F32), jax.ShapeDtypeStruct((B, G, 8, 128), jnp.int32)],
        compiler_params=_params("parallel", "parallel"),
        name="nsa_decode_select",
    )(_alibi_slopes(n_heads), q, cmp_kv, cmp_kv, overlap)


def _pick_group(ref, g, n_groups):
    out = ref[:, 0, :]
    for gg in range(1, n_groups):
        out = jnp.where(g == gg, ref[:, gg, :], out)
    return out


def _nsa_attend_kernel(pool_blk_ref, blk_ref, slopes_ref, q_ref, gates_ref, ocmp_ref, ksn_ref, vsn_ref,
                       kwn_ref, vwn_ref, kw_ref, vw_ref, *rest, n_groups, gs, pos):
    cache_refs, o_ref = rest[:-1], rest[-1]
    b = pl.program_id(0)
    g = pl.program_id(1)
    dh = HEAD_DIM
    n_sel = len(cache_refs) // 2
    qs = _split_heads(q_ref[0], gs)
    slope = _group_slopes(slopes_ref, g, gs)

    def attend(k_past, v_past, dist, ok, k_new, v_new):
        s = jnp.where(ok, _dot_nt(qs, k_past) * SCALE - slope * dist.astype(F32), NEG)
        s_new = _dot_nt(qs, jnp.broadcast_to(k_new, (gs, dh)))[:, 0:1] * SCALE
        m = jnp.maximum(jnp.max(s, axis=-1, keepdims=True), s_new)
        p = jnp.exp(s - m)
        p_new = jnp.exp(s_new - m)
        den = jnp.sum(p, axis=-1, keepdims=True) + p_new
        acc = _dot(p, v_past) + _mx(p_new).astype(F32) * _mx(v_new).astype(F32)
        return acc / den

    lane = lax.broadcasted_iota(jnp.int32, (1, SEL_BLOCK), 1)
    dists = [pos - (blk_ref[(b * n_groups + g) * n_sel + j] * SEL_BLOCK + lane) for j in range(n_sel)]
    dist = jnp.concatenate(dists, axis=1)
    k_sel = jnp.concatenate([_pick_group(cache_refs[2 * j], g, n_groups) for j in range(n_sel)], axis=0)
    v_sel = jnp.concatenate([_pick_group(cache_refs[2 * j + 1], g, n_groups) for j in range(n_sel)], axis=0)
    o_sel = attend(k_sel, v_sel, dist, dist >= 0, ksn_ref[0], vsn_ref[0])

    n_buf = kw_ref.shape[0]
    dist_w = n_buf - lax.broadcasted_iota(jnp.int32, (1, n_buf), 1)
    o_win = attend(_pick_group(kw_ref, g, n_groups), _pick_group(vw_ref, g, n_groups), dist_w, dist_w < WINDOW,
                   kwn_ref[0], vwn_ref[0])

    gate = jax.nn.sigmoid(gates_ref[0, 0])
    o = gate[:, 0:1] * ocmp_ref[0, 0] + gate[:, 1:2] * o_sel + gate[:, 2:3] * o_win
    for s in range(gs):
        o_ref[0, :, s * dh:(s + 1) * dh] = o[s:s + 1].astype(o_ref.dtype)


def _nsa_decode_attend(q, gates, o_cmp, nsa_new, win_new, cache, win_state, layer, page_table, top_idx,
                       n_heads, n_groups, pos):
    B, n_pages = page_table.shape
    page = cache.shape[2]
    assert cache.shape[3] == 4
    G = n_groups
    gs = n_heads // G
    dh = HEAD_DIM
    per_page = page // SEL_BLOCK
    n_buf = win_state.shape[2]
    assert page % SEL_BLOCK == 0 and pos >= n_buf
    blk = top_idx[:, :, 0, :SEL_TOPK]
    phys = jnp.take_along_axis(page_table[:, None, :], (blk // per_page).reshape(B, 1, -1), axis=2)
    pool_blk = phys.reshape(B, G, SEL_TOPK) * per_page + blk % per_page
    n_sel = SEL_TOPK

    def cache_spec(j, kind):
        def imap(b, g, pool_blk_ref, blk_ref):
            pb = pool_blk_ref[(b * G + g) * n_sel + j]
            return (layer, pb // per_page, pb % per_page, kind, 0, 0)
        return pl.BlockSpec((None, None, SEL_BLOCK, None, G, dh), imap)

    cache_specs = []
    for j in range(n_sel):
        cache_specs += [cache_spec(j, 2), cache_spec(j, 3)]
    tok = lambda col0: pl.BlockSpec((1, 1, dh), lambda b, g, *_: (b, 0, col0 + g))
    win = lambda kv: pl.BlockSpec((None, None, n_buf, None, G, dh), lambda b, g, *_: (layer, b, 0, kv, 0, 0))
    grid_spec = pltpu.PrefetchScalarGridSpec(
        num_scalar_prefetch=2,
        grid=(B, G),
        in_specs=[pl.BlockSpec(memory_space=pltpu.SMEM),
                  pl.BlockSpec((1, 1, gs * dh), lambda b, g, *_: (b, 0, g)),
                  pl.BlockSpec((1, 1, gs, 3), lambda b, g, *_: (b, g, 0, 0)),
                  pl.BlockSpec((1, 1, gs, dh), lambda b, g, *_: (b, g, 0, 0)),
                  tok(2 * G), tok(3 * G), tok(0), tok(G), win(0), win(1)] + cache_specs,
        out_specs=pl.BlockSpec((1, 1, gs * dh), lambda b, g, *_: (b, 0, g)))
    return pl.pallas_call(
        functools.partial(_nsa_attend_kernel, n_groups=G, gs=gs, pos=pos),
        grid_spec=grid_spec,
        out_shape=jax.ShapeDtypeStruct((B, 1, n_heads * dh), BF16),
        compiler_params=_params("parallel", "arbitrary"),
        name="nsa_decode_attend",
    )(pool_blk.reshape(-1), blk.reshape(-1), _alibi_slopes(n_heads), q, gates, o_cmp,
      nsa_new, nsa_new, win_new, win_new, win_state, win_state, *([cache] * (2 * n_sel)))


def _trunk(x3, start, past, ln_g, ln_b, ffn_gate, ffn_up, ffn_down, attn_w_in, attn_w_out,
           cmp_pe, cmp_w1, cmp_w2, pool_w, pool_scale):
    B, T, D = x3.shape
    M = B * T
    depth = ffn_gate.shape[0]
    d_ff = ffn_gate.shape[-1]
    alpha = (2.0 * depth) ** 0.25
    decode = past is not None
    dh = HEAD_DIM
    n_heads = D // dh
    H_A = n_heads // 2
    NH = n_heads - H_A
    G = cmp_pe.shape[1]
    gs = NH // G
    moba_w, nsa_w, kvw = H_A * dh, NH * dh, G * dh
    tm = min(M, 1024)
    tm_ln = min(M, 256)
    pool_w_mx = pool_w.astype(MXU_DTYPE)
    if decode:
        assert T == 1
        cache_moba_kv, cache_nsa_kv, state_nsa_win, state_pool, page_table = past

    x = x3.reshape(M, D)
    xb = x.astype(BF16)
    new_moba, new_nsa, new_win, new_pool = [], [], [], []

    def ffn(x, xb, layer, sub, ln_idx):
        hidden = _ffn_up(xb, ffn_gate, ffn_up, layer, sub, tm)
        y = _matmul(hidden, ffn_down, (layer, sub), 0, D, F32, tm=tm, tn=256, kchunk=d_ff // 2, name="ffn_down")
        return _res_ln(y, x, ln_g[layer, ln_idx], ln_b[layer, ln_idx], alpha=alpha, coef=0.5, tm=tm_ln)

    for layer in range(depth):
        x, xb = ffn(x, xb, layer, 0, 0)
        if layer % 2 == 0:
            a = layer // 2
            proj = functools.partial(_matmul, xb, attn_w_in, (a,), tm=tm, name="in_proj")
            q_a = proj(0, moba_w, BF16, tn=256)
            moba_kv = proj(moba_w, 2 * moba_w, F32, tn=256)
            q_b = proj(3 * moba_w, nsa_w, BF16, tn=256)
            c0 = 3 * moba_w + nsa_w
            nsa_kv = proj(c0, 4 * kvw, F32, tn=256)
            win_kv = proj(c0 + 4 * kvw, 2 * kvw, F32, tn=256)
            gates = proj(c0 + 6 * kvw, 3 * NH, F32, tn=128)
            moba_new = moba_kv.reshape(B, T, 2, H_A, dh)
            nsa_new = nsa_kv.reshape(B, T, 4, G, dh)
            win_new = win_kv.reshape(B, T, 2, G, dh)
            q_a3, q_b3 = q_a.reshape(B, T, moba_w), q_b.reshape(B, T, nsa_w)
            moba_kv3, nsa_kv3, win_kv3 = (moba_kv.reshape(B, T, 2 * moba_w), nsa_kv.reshape(B, T, 4 * kvw),
                                          win_kv.reshape(B, T, 2 * kvw))
            if not decode:
                o_a = _moba_prompt(q_a3, moba_kv3, H_A)
                kv16 = nsa_new[:, :, 0:2].transpose(0, 2, 3, 1, 4).reshape(B, 2, G, T // 16, 16 * dh)
                cmp_kv = _compress_prompt(kv16, cmp_pe[a].reshape(2, 2, 16 * dh), cmp_w1, cmp_w2, a)
                gates_g = gates.reshape(B, T, 3, G, gs).transpose(0, 3, 1, 2, 4).reshape(B, G, T, 3 * gs)
                o_b = _nsa_prompt(q_b3, gates_g, cmp_kv, nsa_kv3, win_kv3, NH, G)
                win_state = win_new[:, -min(WINDOW, T):]
            else:
                top_a = _moba_decode_select(q_a3, cache_moba_kv, a, page_table, H_A)
                o_a = _moba_decode_attend(q_a3, moba_kv3, cache_moba_kv, a, page_table, top_a, H_A, start)
                cmp_kv = _nsa_decode_compress(cache_nsa_kv, a, page_table, cmp_pe, cmp_w1, cmp_w2, G)
                o_cmp, top_b = _nsa_decode_select(q_b3, cmp_kv, NH, G, start)
                gates_g = gates.reshape(B, 3, G, gs).transpose(0, 2, 3, 1)
                o_b = _nsa_decode_attend(q_b3, gates_g, o_cmp, nsa_kv3, win_kv3, cache_nsa_kv, state_nsa_win, a,
                                         page_table, top_b, NH, G, start)
                win_state = jnp.concatenate([state_nsa_win[a][:, T:], win_new], axis=1)
            mixed = jnp.concatenate([o_a, o_b], axis=-1).reshape(M, D)
            new_moba.append(moba_new)
            new_nsa.append(nsa_new)
            new_win.append(win_state)
            y = _matmul(mixed, attn_w_out, (a,), 0, D, F32, tm=tm, tn=256, name="out_proj")
            x, xb = _res_ln(y, x, ln_g[layer, 1], ln_b[layer, 1], alpha=alpha, coef=1.0, tm=tm_ln)
        else:
            p = layer // 2
            x3c = x.reshape(B, T, D)
            if decode:
                prev = jnp.concatenate([state_pool[p], x3c], axis=1)
                new_pool.append(prev[:, -POOL_BUF:])
                tt = T
            else:
                prev = x3c
                new_pool.append(x3c[:, -POOL_BUF:])
                tt = 256
            xo, xbo = _pool_mixer_ln(x3c, prev, pool_w_mx[p], pool_scale[p], ln_g[layer, 1], ln_b[layer, 1],
                                     alpha=alpha, start=start, tt=tt, decode=decode)
            x, xb = xo.reshape(M, D), xbo.reshape(M, D)
        x, xb = ffn(x, xb, layer, 1, 2)
    return (x.reshape(B, T, D), jnp.stack(new_moba), jnp.stack(new_nsa), jnp.stack(new_win),
            jnp.stack(new_pool))


def kernel(x_prompt, x_sample, cache_moba_kv, cache_nsa_kv, state_nsa_win, state_pool, page_table,
           ln_g, ln_b, ffn_gate, ffn_up, ffn_down, attn_w_in, attn_w_out, cmp_pe, cmp_w1, cmp_w2,
           pool_w, pool_scale):
    weights = (ln_g, ln_b, ffn_gate, ffn_up, ffn_down, attn_w_in, attn_w_out, cmp_pe, cmp_w1, cmp_w2,
               pool_w, pool_scale)
    past_len = page_table.shape[1] * cache_moba_kv.shape[2]
    y_p, moba_p, nsa_p, win_p, pool_p = _trunk(x_prompt, 0, None, *weights)
    y_s, moba_s, nsa_s, win_s, pool_s = _trunk(
        x_sample, past_len, (cache_moba_kv, cache_nsa_kv, state_nsa_win, state_pool, page_table), *weights)
    return (y_p, y_s, moba_p, nsa_p, win_p, pool_p, moba_s, nsa_s, win_s, pool_s)
```

```python
import functools

import jax
import jax.numpy as jnp
import numpy as np
from jax import lax
from jax.experimental import pallas as pl
from jax.experimental.pallas import tpu as pltpu

F32 = jnp.float32
BF16 = jnp.bfloat16
MXU_DTYPE = BF16

HEAD_DIM = 128
MOBA_BLOCK = 256
MOBA_TOPK = 3
CMP_LEN = 32
CMP_STRIDE = 16
SEL_BLOCK = 64
SEL_TOPK = 15
WINDOW = 512
POOL_WINDOWS = (2, 4, 8, 16)
POOL_BUF = max(POOL_WINDOWS) - 1
LN_EPS = 1e-5
SCALE = HEAD_DIM ** -0.5
NEG = -1e30
LOWEST = -3e38
NSA_Q_TILE = 256
NSA_KEY_CHUNK = 1024
MOBA_KEY_CHUNK = 1024
SUBLANES = 8
DECODE_PAGES_PER_STEP = 4

V7X_VMEM_BYTES = 64 * 1024 * 1024
VMEM_LIMIT = V7X_VMEM_BYTES - 8 * 1024 * 1024


def _params(*sem):
    return pltpu.CompilerParams(dimension_semantics=sem, vmem_limit_bytes=VMEM_LIMIT)


def _mx(a):
    return a.astype(MXU_DTYPE)


def _dot(a, b):
    return jnp.dot(_mx(a), _mx(b), preferred_element_type=F32)


def _dot_nt(a, b):
    return lax.dot_general(_mx(a), _mx(b), (((1,), (1,)), ((), ())), preferred_element_type=F32)


def _alibi_slopes(n):
    return jnp.asarray(np.exp2(-8.0 * np.arange(1, n + 1) / n), dtype=F32)


def _layernorm_rows(z, g, b):
    mu = jnp.mean(z, axis=-1, keepdims=True)
    zc = z - mu
    var = jnp.mean(zc * zc, axis=-1, keepdims=True)
    return zc * lax.rsqrt(var + LN_EPS) * g + b


def _ffn_up_kernel(x_ref, wg_ref, wu_ref, o_ref):
    x = x_ref[...]
    g = _dot(x, wg_ref[...])
    u = _dot(x, wu_ref[...])
    o_ref[...] = (g * jax.nn.sigmoid(g) * u).astype(o_ref.dtype)


def _ffn_up(xb, w_gate, w_up, layer, sub, tm, tn=256):
    M, D = xb.shape
    F = w_gate.shape[-1]
    assert M % tm == 0 and F % tn == 0
    wspec = pl.BlockSpec((None, None, D, tn), lambda i, j: (layer, sub, 0, j))
    return pl.pallas_call(
        _ffn_up_kernel,
        grid=(M // tm, F // tn),
        in_specs=[pl.BlockSpec((tm, D), lambda i, j: (i, 0)), wspec, wspec],
        out_specs=pl.BlockSpec((tm, tn), lambda i, j: (i, j)),
        out_shape=jax.ShapeDtypeStruct((M, F), BF16),
        compiler_params=_params("parallel", "arbitrary"),
        name="ffn_up",
    )(xb, w_gate, w_up)


def _matmul_kernel(x_ref, w_ref, o_ref, *, kchunk):
    K = x_ref.shape[1]
    acc = None
    for k0 in range(0, K, kchunk):
        part = _dot(x_ref[:, k0:k0 + kchunk], w_ref[k0:k0 + kchunk, :])
        acc = part if acc is None else acc + part
    o_ref[...] = acc[:, :o_ref.shape[1]].astype(o_ref.dtype)


def _matmul(xb, w, widx, col0, ncols, out_dtype, *, tm, tn, kchunk=None, name):
    M, K = xb.shape
    assert M % tm == 0 and col0 % tn == 0
    kchunk = K if kchunk is None else kchunk
    assert K % kchunk == 0
    if ncols % tn == 0:
        nj, out_tn = ncols // tn, tn
    else:
        assert ncols < tn
        nj, out_tn = 1, ncols
    j0 = col0 // tn
    lead = (None,) * len(widx)
    x_spec = pl.BlockSpec((tm, K), lambda i, j: (i, 0), pipeline_mode=pl.Buffered(1))
    return pl.pallas_call(
        functools.partial(_matmul_kernel, kchunk=kchunk),
        grid=(M // tm, nj),
        in_specs=[x_spec, pl.BlockSpec(lead + (K, tn), lambda i, j: tuple(widx) + (0, j0 + j))],
        out_specs=pl.BlockSpec((tm, out_tn), lambda i, j: (i, j)),
        out_shape=jax.ShapeDtypeStruct((M, ncols), out_dtype),
        compiler_params=_params("parallel", "arbitrary"),
        name=name,
    )(xb, w)


def _res_ln_kernel(y_ref, res_ref, g_ref, b_ref, o_ref, ob_ref, *, alpha, coef):
    z = alpha * res_ref[...] + coef * y_ref[...]
    y = _layernorm_rows(z, g_ref[...], b_ref[...])
    o_ref[...] = y
    ob_ref[...] = y.astype(ob_ref.dtype)


def _res_ln(y, res, g, b, *, alpha, coef, tm):
    M, D = y.shape
    assert M % tm == 0
    row = pl.BlockSpec((1, D), lambda i: (0, 0))
    tile = pl.BlockSpec((tm, D), lambda i: (i, 0))
    return pl.pallas_call(
        functools.partial(_res_ln_kernel, alpha=alpha, coef=coef),
        grid=(M // tm,),
        in_specs=[tile, tile, row, row],
        out_specs=[tile, tile],
        out_shape=[jax.ShapeDtypeStruct((M, D), F32), jax.ShapeDtypeStruct((M, D), BF16)],
        compiler_params=_params("parallel"),
        name="res_ln",
    )(y, res, g.reshape(1, D), b.reshape(1, D))


def _pool_kernel(x_ref, prev_ref, w_ref, sc_ref, g_ref, b_ref, o_ref, ob_ref, *,
                 alpha, start, tt, decode, group):
    ti = pl.program_id(1)
    zs = []
    for gi, win in enumerate(POOL_WINDOWS):
        c0, c1 = gi * group, (gi + 1) * group
        prev = prev_ref[0, :, c0:c1]
        if decode:
            xg = prev[POOL_BUF:POOL_BUF + 1]
            wsum = jnp.sum(prev[POOL_BUF + 1 - win:], axis=0, keepdims=True)
            cnt = jnp.full((1, 1), float(min(win, start + 1)), F32)
        else:
            xg = x_ref[0, :, c0:c1]
            prev = jnp.where(ti == 0, 0.0, prev)
            s = jnp.concatenate([prev, xg], axis=0)
            span = 1
            while span < win:
                s = s + pltpu.roll(s, span, axis=0)
                span *= 2
            wsum = s[POOL_BUF + 1:]
            pos = start + ti * tt + lax.broadcasted_iota(jnp.int32, (tt, 1), 0)
            cnt = jnp.minimum(win, pos + 1).astype(F32)
        pooled = wsum / cnt - xg
        y = _dot(pooled, w_ref[gi]) * sc_ref[:, c0:c1]
        zs.append(alpha * xg + y)
    d = group * len(POOL_WINDOWS)
    mu = sum(jnp.sum(z, axis=-1, keepdims=True) for z in zs) / d
    var = sum(jnp.sum((z - mu) * (z - mu), axis=-1, keepdims=True) for z in zs) / d
    inv = lax.rsqrt(var + LN_EPS)
    for gi, z in enumerate(zs):
        c0, c1 = gi * group, (gi + 1) * group
        y = (z - mu) * inv * g_ref[:, c0:c1] + b_ref[:, c0:c1]
        o_ref[0, :, c0:c1] = y
        ob_ref[0, :, c0:c1] = y.astype(ob_ref.dtype)


def _pool_mixer_ln(x3, prev3, w_pool, scale, g, b, *, alpha, start, tt, decode):
    B, T, D = x3.shape
    group = D // len(POOL_WINDOWS)
    assert T % tt == 0 and (decode or tt % 16 == 0)
    per = tt // 16 if not decode else 0
    if decode:
        prev_map = lambda bi, ti: (bi, 0, 0)
    else:
        prev_map = lambda bi, ti: (bi, jnp.maximum(ti * per - 1, 0), 0)
    row = pl.BlockSpec((1, D), lambda bi, ti: (0, 0))
    tile = pl.BlockSpec((1, tt, D), lambda bi, ti: (bi, ti, 0))
    return pl.pallas_call(
        functools.partial(_pool_kernel, alpha=alpha, start=start, tt=tt, decode=decode, group=group),
        grid=(B, T // tt),
        in_specs=[tile, pl.BlockSpec((1, 16, D), prev_map),
                  pl.BlockSpec(w_pool.shape, lambda bi, ti: (0, 0, 0)), row, row, row],
        out_specs=[tile, tile],
        out_shape=[jax.ShapeDtypeStruct((B, T, D), F32), jax.ShapeDtypeStruct((B, T, D), BF16)],
        compiler_params=_params("parallel", "arbitrary"),
        name="pool_mixer_ln",
    )(x3, prev3, w_pool, scale.reshape(1, D), g.reshape(1, D), b.reshape(1, D))


def _topk_mask(score, col, n_valid, k):
    n = score.shape[1]
    masked = jnp.where(col < n_valid, score, NEG)
    rank = jnp.zeros(score.shape, jnp.int32)
    for j in range(n):
        other = masked[:, j:j + 1]
        ahead = (other > masked) | ((other == masked) & (j < col))
        rank = rank + ahead.astype(jnp.int32)
    return (rank < k) & (col < n_valid)


def _online_update(m_ref, l_ref, acc_ref, s, v):
    m_old = m_ref[...]
    m_new = jnp.maximum(m_old, jnp.max(s, axis=-1, keepdims=True))
    p = jnp.exp(s - m_new)
    a = jnp.exp(m_old - m_new)
    l_ref[...] = a * l_ref[...] + jnp.sum(p, axis=-1, keepdims=True)
    acc_ref[...] = a * acc_ref[...] + _dot(p, v)
    m_ref[...] = m_new


def _moba_kernel(slopes_ref, q_ref, k_ref, v_ref, ex_ref, o_ref, m_ref, l_ref, acc_ref, *, nb):
    h = pl.program_id(1)
    qi = pl.program_id(2)
    blk = MOBA_BLOCK
    T = nb * blk
    kc = min(T, MOBA_KEY_CHUNK)
    slope = slopes_ref[h]
    q = q_ref[0]
    kmean = jnp.concatenate(
        [jnp.sum(k_ref[0, n * blk:(n + 1) * blk, :], axis=0, keepdims=True) for n in range(nb)],
        axis=0) * (1.0 / blk)
    gate = _dot_nt(q, kmean)
    col = lax.broadcasted_iota(jnp.int32, gate.shape, 1)
    sel = _topk_mask(gate, col, qi, MOBA_TOPK) | (col == qi)
    sel_keys = _dot(sel.astype(F32), ex_ref[...])
    qpos = qi * blk + lax.broadcasted_iota(jnp.int32, (blk, 1), 0)

    m_ref[...] = jnp.full(m_ref.shape, NEG, F32)
    l_ref[...] = jnp.zeros(l_ref.shape, F32)
    acc_ref[...] = jnp.zeros(acc_ref.shape, F32)
    for c in reversed(range(T // kc)):
        @pl.when(c * kc <= qi * blk)
        def _():
            dist = qpos - (c * kc + lax.broadcasted_iota(jnp.int32, (1, kc), 1))
            ok = (sel_keys[:, c * kc:(c + 1) * kc] > 0.5) & (dist >= 0)
            s = (_dot_nt(q, k_ref[0, c * kc:(c + 1) * kc, :]) * SCALE - slope * dist.astype(F32)
                 + jnp.where(ok, 0.0, NEG))
            _online_update(m_ref, l_ref, acc_ref, s, v_ref[0, c * kc:(c + 1) * kc, :])
    o_ref[0] = (acc_ref[...] / l_ref[...]).astype(o_ref.dtype)


def _moba_prompt(q, kv, n_heads):
    B, T, _ = q.shape
    assert T % MOBA_BLOCK == 0 and T % min(T, MOBA_KEY_CHUNK) == 0 and MOBA_KEY_CHUNK % MOBA_BLOCK == 0
    nb = T // MOBA_BLOCK
    dh = HEAD_DIM
    expand = (np.arange(T)[None, :] // MOBA_BLOCK) == np.arange(nb)[:, None]
    expand = jnp.asarray(expand.astype(np.float32), BF16)
    return pl.pallas_call(
        functools.partial(_moba_kernel, nb=nb),
        grid=(B, n_heads, nb),
        in_specs=[pl.BlockSpec(memory_space=pltpu.SMEM),
                  pl.BlockSpec((1, MOBA_BLOCK, dh), lambda b, h, i: (b, i, h)),
                  pl.BlockSpec((1, T, dh), lambda b, h, i: (b, 0, h)),
                  pl.BlockSpec((1, T, dh), lambda b, h, i: (b, 0, n_heads + h)),
                  pl.BlockSpec((nb, T), lambda b, h, i: (0, 0))],
        out_specs=pl.BlockSpec((1, MOBA_BLOCK, dh), lambda b, h, i: (b, i, h)),
        out_shape=jax.ShapeDtypeStruct((B, T, n_heads * dh), BF16),
        scratch_shapes=[pltpu.VMEM((MOBA_BLOCK, 1), F32), pltpu.VMEM((MOBA_BLOCK, 1), F32),
                        pltpu.VMEM((MOBA_BLOCK, dh), F32)],
        compiler_params=_params("parallel", "parallel", "arbitrary"),
        name="moba_prompt",
    )(_alibi_slopes(n_heads), q, kv, kv, expand)


def _compress_kernel(x_ref, pe_ref, w1_ref, w2_ref, o_ref, *, nc):
    x = x_ref[0, 0, 0]
    half = x.shape[1]
    lo = _dot(x + pe_ref[0, 0:1, :], w1_ref[:half, :])
    hi = _dot(x + pe_ref[0, 1:2, :], w1_ref[half:, :])
    rows = x.shape[0]
    hid = lo + pltpu.roll(hi, rows - 1, axis=0)
    y = _dot(hid * jax.nn.sigmoid(hid), w2_ref[...])
    ridx = lax.broadcasted_iota(jnp.int32, y.shape, 0)
    o_ref[0, 0, 0] = jnp.where(ridx < nc, y, 0.0)


def _compress_prompt(kv16, pe, w1, w2, layer):
    B, two, G, R, W = kv16.shape
    nc = R - 1
    hidden = w1.shape[-1]
    return pl.pallas_call(
        functools.partial(_compress_kernel, nc=nc),
        grid=(B, two, G),
        in_specs=[pl.BlockSpec((1, 1, 1, R, W), lambda b, c, g: (b, c, g, 0, 0)),
                  pl.BlockSpec((1, 2, W), lambda b, c, g: (c, 0, 0)),
                  pl.BlockSpec((None, None, 2 * W, hidden), lambda b, c, g: (layer, c, 0, 0)),
                  pl.BlockSpec((None, None, hidden, HEAD_DIM), lambda b, c, g: (layer, c, 0, 0))],
        out_specs=pl.BlockSpec((1, 1, 1, R, HEAD_DIM), lambda b, c, g: (b, c, g, 0, 0)),
        out_shape=jax.ShapeDtypeStruct((B, two, G, R, HEAD_DIM), F32),
        compiler_params=_params("parallel", "parallel", "parallel"),
        name="nsa_compress",
    )(kv16, pe, w1, w2)


def _nsa_kernel(slopes_ref, q_ref, gates_ref, kc_ref, vc_ref, ks_ref, vs_ref, kw_ref, vw_ref,
                ov_ref, ex_ref, o_ref, m_ref, l_ref, acc_ref, *, T, gs):
    g = pl.program_id(1)
    qi = pl.program_id(2)
    tq = NSA_Q_TILE
    dh = HEAD_DIM
    q = q_ref[0]
    qs = jnp.concatenate([q[:, s * dh:(s + 1) * dh] for s in range(gs)], axis=0)
    slopes = [slopes_ref[g * gs + s] for s in range(gs)]
    heads = [(s * tq, (s + 1) * tq) for s in range(gs)]
    tpos = qi * tq + lax.broadcasted_iota(jnp.int32, (tq, 1), 0)

    ncp = kc_ref.shape[3]
    c_end = lax.broadcasted_iota(jnp.int32, (1, ncp), 1) * CMP_STRIDE + (CMP_LEN - 1)
    dist_c = tpos - c_end
    cmask = dist_c >= 0
    bias_c = jnp.where(cmask, 0.0, NEG)
    distf_c = dist_c.astype(F32)
    s_all = _dot_nt(qs, kc_ref[0, 0, 0])
    p_list = []
    for s, (r0, r1) in enumerate(heads):
        sc = s_all[r0:r1] * SCALE - slopes[s] * distf_c + bias_c
        e = jnp.where(cmask, jnp.exp(sc - jnp.max(sc, axis=-1, keepdims=True)), 0.0)
        den = jnp.sum(e, axis=-1, keepdims=True)
        p_list.append(e / jnp.where(den > 0.0, den, 1.0))
    p_c = _mx(jnp.concatenate(p_list, axis=0))
    o_cmp = _dot(p_c, vc_ref[0, 0, 0])

    imp_rows = _dot(p_c, ov_ref[...])
    imp = imp_rows[0:tq]
    for r0, r1 in heads[1:]:
        imp = imp + imp_rows[r0:r1]
    ns = imp.shape[1]
    colj = lax.broadcasted_iota(jnp.int32, (tq, ns), 1)
    sel = _topk_mask(imp, colj, tpos // SEL_BLOCK, SEL_TOPK).astype(F32)
    sel_keys = _dot(sel, ex_ref[...])

    m_ref[...] = jnp.full(m_ref.shape, NEG, F32)
    l_ref[...] = jnp.zeros(l_ref.shape, F32)
    acc_ref[...] = jnp.zeros(acc_ref.shape, F32)
    kc = NSA_KEY_CHUNK
    for c in reversed(range(T // kc)):
        @pl.when(c * kc < (qi + 1) * tq)
        def _():
            kpos = c * kc + lax.broadcasted_iota(jnp.int32, (1, kc), 1)
            dist = tpos - kpos
            own = (kpos // SEL_BLOCK) == (tpos // SEL_BLOCK)
            ok = ((sel_keys[:, c * kc:(c + 1) * kc] > 0.5) | own) & (dist >= 0)
            bias = jnp.where(ok, 0.0, NEG)
            distf = dist.astype(F32)
            s_all = _dot_nt(qs, ks_ref[0, c * kc:(c + 1) * kc, :])
            m_all = m_ref[...]
            ps, maxes, sums = [], [], []
            for s, (r0, r1) in enumerate(heads):
                sc = s_all[r0:r1] * SCALE - slopes[s] * distf + bias
                m_new = jnp.maximum(m_all[r0:r1], jnp.max(sc, axis=-1, keepdims=True))
                p = jnp.exp(sc - m_new)
                maxes.append(m_new)
                sums.append(jnp.sum(p, axis=-1, keepdims=True))
                ps.append(_mx(p))
            m_new = jnp.concatenate(maxes, axis=0)
            a = jnp.exp(m_all - m_new)
            pv = _dot(jnp.concatenate(ps, axis=0), vs_ref[0, c * kc:(c + 1) * kc, :])
            l_ref[...] = a * l_ref[...] + jnp.concatenate(sums, axis=0)
            acc_ref[...] = a * acc_ref[...] + pv
            m_ref[...] = m_new
    o_sel = acc_ref[...] / l_ref[...]

    span = WINDOW + tq
    w0 = pl.multiple_of(jnp.maximum(qi * tq - WINDOW, 0), tq)
    kpos = w0 + lax.broadcasted_iota(jnp.int32, (1, span), 1)
    dist = tpos - kpos
    bias = jnp.where((dist >= 0) & (dist < WINDOW), 0.0, NEG)
    distf = dist.astype(F32)
    s_all = _dot_nt(qs, kw_ref[0, pl.ds(w0, span), :])
    ps = []
    for s, (r0, r1) in enumerate(heads):
        sc = s_all[r0:r1] * SCALE - slopes[s] * distf + bias
        e = jnp.exp(sc - jnp.max(sc, axis=-1, keepdims=True))
        ps.append(_mx(e / jnp.sum(e, axis=-1, keepdims=True)))
    o_win = _dot(jnp.concatenate(ps, axis=0), vw_ref[0, pl.ds(w0, span), :])

    gate = jax.nn.sigmoid(gates_ref[0, 0])
    for s, (r0, r1) in enumerate(heads):
        o = (gate[:, s:s + 1] * o_cmp[r0:r1] + gate[:, gs + s:gs + s + 1] * o_sel[r0:r1]
             + gate[:, 2 * gs + s:2 * gs + s + 1] * o_win[r0:r1])
        o_ref[0, :, s * dh:(s + 1) * dh] = o.astype(o_ref.dtype)


def _nsa_tables(T, rows_cmp):
    ns = -(-T // SEL_BLOCK)
    c_start = np.arange(rows_cmp) * CMP_STRIDE
    s_start = np.arange(ns) * SEL_BLOCK
    nc = (T - CMP_LEN) // CMP_STRIDE + 1
    overlap = ((c_start[:, None] <= s_start[None, :] + SEL_BLOCK - 1)
               & (c_start[:, None] + CMP_LEN - 1 >= s_start[None, :])
               & (np.arange(rows_cmp)[:, None] < nc))
    expand = (np.arange(T)[None, :] // SEL_BLOCK) == np.arange(ns)[:, None]
    return jnp.asarray(overlap.astype(np.float32), BF16), jnp.asarray(expand.astype(np.float32), BF16)


def _nsa_prompt(q, gates, cmp_kv, nsa_kv, win_kv, n_heads, n_groups):
    B, T, _ = q.shape
    G = n_groups
    gs = n_heads // G
    dh = HEAD_DIM
    tq = NSA_Q_TILE
    assert T % NSA_KEY_CHUNK == 0 and T >= WINDOW + tq and T % tq == 0
    R = cmp_kv.shape[3]
    overlap, expand = _nsa_tables(T, R)
    ns = overlap.shape[1]
    rows = gs * tq
    seq = lambda col: pl.BlockSpec((1, T, dh), lambda b, g, i: (b, 0, col(g)))
    return pl.pallas_call(
        functools.partial(_nsa_kernel, T=T, gs=gs),
        grid=(B, G, T // tq),
        in_specs=[pl.BlockSpec(memory_space=pltpu.SMEM),
                  pl.BlockSpec((1, tq, gs * dh), lambda b, g, i: (b, i, g)),
                  pl.BlockSpec((1, 1, tq, 3 * gs), lambda b, g, i: (b, g, i, 0)),
                  pl.BlockSpec((1, 1, 1, R, dh), lambda b, g, i: (b, 0, g, 0, 0)),
                  pl.BlockSpec((1, 1, 1, R, dh), lambda b, g, i: (b, 1, g, 0, 0)),
                  seq(lambda g: 2 * G + g), seq(lambda g: 3 * G + g),
                  seq(lambda g: g), seq(lambda g: G + g),
                  pl.BlockSpec((R, ns), lambda b, g, i: (0, 0)),
                  pl.BlockSpec((ns, T), lambda b, g, i: (0, 0))],
        out_specs=pl.BlockSpec((1, tq, gs * dh), lambda b, g, i: (b, i, g)),
        out_shape=jax.ShapeDtypeStruct((B, T, n_heads * dh), BF16),
        scratch_shapes=[pltpu.VMEM((rows, 1), F32), pltpu.VMEM((rows, 1), F32), pltpu.VMEM((rows, dh), F32)],
        compiler_params=_params("parallel", "parallel", "arbitrary"),
        name="nsa_prompt",
    )(_alibi_slopes(n_heads), q, gates, cmp_kv, cmp_kv, nsa_kv, nsa_kv, win_kv, win_kv, overlap, expand)


def _first_argmax(score, lane):
    mx = jnp.max(score, axis=-1, keepdims=True)
    return jnp.min(jnp.where(score == mx, lane, float(score.shape[1])), axis=-1, keepdims=True)


def _topk_indices(score, k, width):
    lane = lax.broadcasted_iota(jnp.int32, score.shape, 1).astype(F32)
    out_lane = lax.broadcasted_iota(jnp.int32, (score.shape[0], width), 1)
    out = jnp.zeros((score.shape[0], width), jnp.int32)
    for j in range(k):
        idx = _first_argmax(score, lane)
        out = jnp.where(out_lane == j, idx.astype(jnp.int32), out)
        score = jnp.where(lane == idx, LOWEST, score)
    return out


def _moba_select_kernel(pt_ref, q_ref, *rest, n_heads, n_steps, pages_per_block):
    cache_refs, (idx_ref, ksum_ref) = rest[:-2], rest[-2:]
    st = pl.program_id(1)
    blocks_per_step = len(cache_refs) // pages_per_block
    for i in range(blocks_per_step):
        total = None
        for r in range(pages_per_block):
            part = jnp.sum(cache_refs[i * pages_per_block + r][...], axis=0, keepdims=True)
            total = part if total is None else total + part
        ksum_ref[pl.ds(st * blocks_per_step + i, 1)] = total

    @pl.when(st == n_steps - 1)
    def _():
        dh = HEAD_DIM
        q = q_ref[0]
        gate = jnp.concatenate(
            [_dot_nt(q[:, h * dh:(h + 1) * dh], ksum_ref[:, h, :] * (1.0 / MOBA_BLOCK)) for h in range(n_heads)],
            axis=0)
        idx_ref[0] = _topk_indices(gate, MOBA_TOPK, idx_ref.shape[2])


def _moba_decode_select(q, cache, layer, page_table, n_heads):
    B, n_pages = page_table.shape
    page = cache.shape[2]
    width = n_heads * HEAD_DIM
    ppb = MOBA_BLOCK // page
    pps = DECODE_PAGES_PER_STEP
    assert MOBA_BLOCK % page == 0 and pps % ppb == 0 and n_pages % pps == 0 and n_pages // ppb >= MOBA_TOPK
    n_steps = n_pages // pps

    def cache_spec(i):
        return pl.BlockSpec((None, None, page, None, n_heads, HEAD_DIM),
                            lambda b, st, pt: (layer, pt[b * n_pages + st * pps + i], 0, 0, 0, 0))

    grid_spec = pltpu.PrefetchScalarGridSpec(
        num_scalar_prefetch=1,
        grid=(B, n_steps),
        in_specs=[pl.BlockSpec((1, 1, width), lambda b, st, pt: (b, 0, 0))] + [cache_spec(i) for i in range(pps)],
        out_specs=pl.BlockSpec((1, n_heads, 128), lambda b, st, pt: (b, 0, 0)),
        scratch_shapes=[pltpu.VMEM((n_pages // ppb, n_heads, HEAD_DIM), F32)])
    return pl.pallas_call(
        functools.partial(_moba_select_kernel, n_heads=n_heads, n_steps=n_steps, pages_per_block=ppb),
        grid_spec=grid_spec,
        out_shape=jax.ShapeDtypeStruct((B, n_heads, 128), jnp.int32),
        compiler_params=_params("parallel", "arbitrary"),
        name="moba_decode_select",
    )(page_table.reshape(-1), q, *([cache] * pps))


def _moba_attend_kernel(phys_ref, blk_ref, slopes_ref, q_ref, kn_ref, vn_ref, *rest, n_heads, pos, page, ppb):
    cache_refs, o_ref = rest[:-1], rest[-1]
    b = pl.program_id(0)
    h = pl.program_id(1)
    slope = slopes_ref[h]
    q = q_ref[0]
    n_sel = len(cache_refs) // (2 * ppb)
    rows = page * SUBLANES
    ridx = lax.broadcasted_iota(jnp.int32, (1, rows), 1)
    mine = (ridx % SUBLANES) == (h % SUBLANES)
    scores, values = [], []
    for j in range(n_sel):
        blk = blk_ref[(b * n_heads + h) * n_sel + j]
        for r in range(ppb):
            k_ref, v_ref = cache_refs[2 * (j * ppb + r)], cache_refs[2 * (j * ppb + r) + 1]
            dist = pos - (blk * MOBA_BLOCK + r * page + ridx // SUBLANES)
            s = _dot_nt(q, k_ref[...].reshape(rows, HEAD_DIM)) * SCALE - slope * dist.astype(F32)
            scores.append(jnp.where(mine, s, NEG))
            values.append(v_ref)
    s_new = jnp.sum(_mx(q).astype(F32) * _mx(kn_ref[0]).astype(F32), axis=-1, keepdims=True) * SCALE
    m = s_new
    for s in scores:
        m = jnp.maximum(m, jnp.max(s, axis=-1, keepdims=True))
    p_new = jnp.exp(s_new - m)
    den = p_new
    acc = _mx(p_new).astype(F32) * _mx(vn_ref[0]).astype(F32)
    for s, v_ref in zip(scores, values):
        p = jnp.exp(s - m)
        den = den + jnp.sum(p, axis=-1, keepdims=True)
        acc = acc + _dot(p, v_ref[...].reshape(rows, HEAD_DIM))
    o_ref[0] = (acc / den).astype(o_ref.dtype)


def _moba_decode_attend(q, kv_new, cache, layer, page_table, top_idx, n_heads, pos):
    B, n_pages = page_table.shape
    page = cache.shape[2]
    dh = HEAD_DIM
    ppb = MOBA_BLOCK // page
    blk = top_idx[:, :, :MOBA_TOPK]
    pages = blk[..., None] * ppb + jnp.arange(ppb, dtype=jnp.int32)
    phys = jnp.take_along_axis(page_table[:, None, :], pages.reshape(B, 1, -1), axis=2)
    n_sel = MOBA_TOPK

    assert n_heads % SUBLANES == 0
    hb = n_heads // SUBLANES
    cache = cache.reshape(cache.shape[:3] + (2 * hb, SUBLANES, dh))

    def cache_spec(j, r, kv):
        def imap(b, h, phys_ref, blk_ref):
            return (layer, phys_ref[((b * n_heads + h) * n_sel + j) * ppb + r], 0, kv * hb + h // SUBLANES, 0, 0)
        return pl.BlockSpec((None, None, page, None, SUBLANES, dh), imap)

    cache_specs = []
    for j in range(n_sel):
        for r in range(ppb):
            cache_specs += [cache_spec(j, r, 0), cache_spec(j, r, 1)]
    tok = lambda col0: pl.BlockSpec((1, 1, dh), lambda b, h, *_: (b, 0, col0 + h))
    grid_spec = pltpu.PrefetchScalarGridSpec(
        num_scalar_prefetch=2,
        grid=(B, n_heads),
        in_specs=[pl.BlockSpec(memory_space=pltpu.SMEM), tok(0), tok(0), tok(n_heads)] + cache_specs,
        out_specs=tok(0))
    return pl.pallas_call(
        functools.partial(_moba_attend_kernel, n_heads=n_heads, pos=pos, page=page, ppb=ppb),
        grid_spec=grid_spec,
        out_shape=jax.ShapeDtypeStruct((B, 1, n_heads * dh), BF16),
        compiler_params=_params("parallel", "arbitrary"),
        name="moba_decode_attend",
    )(phys.reshape(-1), blk.reshape(-1), _alibi_slopes(n_heads), q, kv_new, kv_new,
      *([cache] * (2 * n_sel * ppb)))


def _nsa_decode_compress_kernel(pt_ref, *refs, n_steps, page, n_groups):
    n_in = len(refs) - 5
    cache_refs = refs[:n_in]
    pe_ref, w1_ref, w2_ref, o_ref, buf_ref = refs[n_in:]
    st = pl.program_id(1)
    pps = n_in // 2
    for i in range(pps):
        row0 = pl.multiple_of((st * pps + i) * page, page)
        for c in range(2 * n_groups):
            buf_ref[c, pl.ds(row0, page), :] = cache_refs[2 * i + c // n_groups][:, c % n_groups, :]

    @pl.when(st == n_steps - 1)
    def _():
        dh = HEAD_DIM
        rows = buf_ref.shape[1] // CMP_STRIDE
        halves = CMP_LEN // CMP_STRIDE
        for c in range(2 * n_groups):
            which = c // n_groups
            parts = [None] * halves
            for l in range(CMP_STRIDE):
                x = buf_ref.at[c][pl.ds(l, rows, stride=CMP_STRIDE), :]
                for hf in range(halves):
                    ll = hf * CMP_STRIDE + l
                    d = _dot(x + pe_ref[which, ll:ll + 1, :], w1_ref[which, ll * dh:(ll + 1) * dh, :])
                    parts[hf] = d if parts[hf] is None else parts[hf] + d
            hid = parts[0]
            for hf in range(1, halves):
                hid = hid + pltpu.roll(parts[hf], rows - hf, axis=0)
            y = _dot(hid * jax.nn.sigmoid(hid), w2_ref[which])
            ridx = lax.broadcasted_iota(jnp.int32, y.shape, 0)
            o_ref[0, c] = jnp.where(ridx < rows - (halves - 1), y, 0.0)


def _nsa_decode_compress(cache, layer, page_table, pe, w1, w2, n_groups):
    B, n_pages = page_table.shape
    page = cache.shape[2]
    dh = HEAD_DIM
    past = n_pages * page
    rows = past // CMP_STRIDE
    hidden = w1.shape[-1]

    pps = DECODE_PAGES_PER_STEP
    assert n_pages % pps == 0
    n_steps = n_pages // pps

    def cache_spec(i, which):
        return pl.BlockSpec((None, None, page, None, n_groups, dh),
                            lambda b, st, pt: (layer, pt[b * n_pages + st * pps + i], 0, which, 0, 0))

    cache_specs = [cache_spec(i, which) for i in range(pps) for which in range(2)]
    grid_spec = pltpu.PrefetchScalarGridSpec(
        num_scalar_prefetch=1,
        grid=(B, n_steps),
        in_specs=cache_specs + [
                  pl.BlockSpec((None, 2, CMP_LEN, dh), lambda b, p, pt: (layer, 0, 0, 0)),
                  pl.BlockSpec((None, 2, CMP_LEN * dh, hidden), lambda b, p, pt: (layer, 0, 0, 0)),
                  pl.BlockSpec((None, 2, hidden, dh), lambda b, p, pt: (layer, 0, 0, 0))],
        out_specs=pl.BlockSpec((1, 2 * n_groups, rows, dh), lambda b, p, pt: (b, 0, 0, 0)),
        scratch_shapes=[pltpu.VMEM((2 * n_groups, past, dh), F32)])
    return pl.pallas_call(
        functools.partial(_nsa_decode_compress_kernel, n_steps=n_steps, page=page, n_groups=n_groups),
        grid_spec=grid_spec,
        out_shape=jax.ShapeDtypeStruct((B, 2 * n_groups, rows, dh), F32),
        compiler_params=_params("parallel", "arbitrary"),
        name="nsa_decode_compress",
    )(page_table.reshape(-1), *([cache] * (2 * pps)), pe, w1, w2)


def _split_heads(q, gs):
    dh = HEAD_DIM
    return jnp.concatenate([q[:, s * dh:(s + 1) * dh] for s in range(gs)], axis=0)


def _group_slopes(slopes_ref, g, gs):
    sub = lax.broadcasted_iota(jnp.int32, (gs, 1), 0)
    slope = jnp.zeros((gs, 1), F32)
    for s in range(gs):
        slope = jnp.where(sub == s, slopes_ref[g * gs + s], slope)
    return slope


def _nsa_select_kernel(slopes_ref, q_ref, kc_ref, vc_ref, ov_ref, ocmp_ref, idx_ref, *, gs, pos):
    g = pl.program_id(1)
    qs = _split_heads(q_ref[0], gs)
    slope = _group_slopes(slopes_ref, g, gs)
    ncp = kc_ref.shape[2]
    c_end = lax.broadcasted_iota(jnp.int32, (1, ncp), 1) * CMP_STRIDE + (CMP_LEN - 1)
    dist_c = pos - c_end
    cmask = dist_c >= 0
    s_c = jnp.where(cmask, _dot_nt(qs, kc_ref[0, 0]) * SCALE - slope * dist_c.astype(F32), NEG)
    e_c = jnp.where(cmask, jnp.exp(s_c - jnp.max(s_c, axis=-1, keepdims=True)), 0.0)
    den = jnp.sum(e_c, axis=-1, keepdims=True)
    p_c = e_c / jnp.where(den > 0.0, den, 1.0)
    ocmp_ref[0, 0] = _dot(p_c, vc_ref[0, 0])
    imp = jnp.sum(_dot(p_c, ov_ref[...]), axis=0, keepdims=True)
    idx = _topk_indices(imp, SEL_TOPK, idx_ref.shape[3])
    idx_ref[0, 0] = jnp.broadcast_to(idx, idx_ref.shape[2:])


def _nsa_decode_select(q, cmp_kv, n_heads, n_groups, pos):
    B = q.shape[0]
    G = n_groups
    gs = n_heads // G
    dh = HEAD_DIM
    R = cmp_kv.shape[2]
    cur = pos // SEL_BLOCK
    assert pos % SEL_BLOCK == 0 and cur >= SEL_TOPK
    c_start = np.arange(R) * CMP_STRIDE
    s_start = np.arange(cur) * SEL_BLOCK
    nc = (pos + 1 - CMP_LEN) // CMP_STRIDE + 1
    overlap = ((c_start[:, None] <= s_start[None, :] + SEL_BLOCK - 1)
               & (c_start[:, None] + CMP_LEN - 1 >= s_start[None, :])
               & (np.arange(R)[:, None] < nc))
    overlap = jnp.asarray(overlap.astype(np.float32), BF16)
    return pl.pallas_call(
        functools.partial(_nsa_select_kernel, gs=gs, pos=pos),
        grid=(B, G),
        in_specs=[pl.BlockSpec(memory_space=pltpu.SMEM),
                  pl.BlockSpec((1, 1, gs * dh), lambda b, g: (b, 0, g)),
                  pl.BlockSpec((1, 1, R, dh), lambda b, g: (b, g, 0, 0)),
                  pl.BlockSpec((1, 1, R, dh), lambda b, g: (b, G + g, 0, 0)),
                  pl.BlockSpec((R, cur), lambda b, g: (0, 0))],
        out_specs=[pl.BlockSpec((1, 1, gs, dh), lambda b, g: (b, g, 0, 0)),
                   pl.BlockSpec((1, 1, 8, 128), lambda b, g: (b, g, 0, 0))],
        out_shape=[jax.ShapeDtypeStruct((B, G, gs, dh), F32), jax.ShapeDtypeStruct((B, G, 8, 128), jnp.int32)],
        compiler_params=_params("parallel", "parallel"),
        name="nsa_decode_select",
    )(_alibi_slopes(n_heads), q, cmp_kv, cmp_kv, overlap)


def _pick_group(ref, g, n_groups):
    out = ref[:, 0, :]
    for gg in range(1, n_groups):
        out = jnp.where(g == gg, ref[:, gg, :], out)
    return out


def _nsa_attend_kernel(pool_blk_ref, blk_ref, slopes_ref, q_ref, gates_ref, ocmp_ref, ksn_ref, vsn_ref,
                       kwn_ref, vwn_ref, kw_ref, vw_ref, *rest, n_groups, gs, pos):
    cache_refs, o_ref = rest[:-1], rest[-1]
    b = pl.program_id(0)
    g = pl.program_id(1)
    dh = HEAD_DIM
    n_sel = len(cache_refs) // 2
    qs = _split_heads(q_ref[0], gs)
    slope = _group_slopes(slopes_ref, g, gs)

    def attend(k_past, v_past, dist, ok, k_new, v_new):
        s = jnp.where(ok, _dot_nt(qs, k_past) * SCALE - slope * dist.astype(F32), NEG)
        s_new = _dot_nt(qs, jnp.broadcast_to(k_new, (gs, dh)))[:, 0:1] * SCALE
        m = jnp.maximum(jnp.max(s, axis=-1, keepdims=True), s_new)
        p = jnp.exp(s - m)
        p_new = jnp.exp(s_new - m)
        den = jnp.sum(p, axis=-1, keepdims=True) + p_new
        acc = _dot(p, v_past) + _mx(p_new).astype(F32) * _mx(v_new).astype(F32)
        return acc / den

    lane = lax.broadcasted_iota(jnp.int32, (1, SEL_BLOCK), 1)
    dists = [pos - (blk_ref[(b * n_groups + g) * n_sel + j] * SEL_BLOCK + lane) for j in range(n_sel)]
    dist = jnp.concatenate(dists, axis=1)
    k_sel = jnp.concatenate([_pick_group(cache_refs[2 * j], g, n_groups) for j in range(n_sel)], axis=0)
    v_sel = jnp.concatenate([_pick_group(cache_refs[2 * j + 1], g, n_groups) for j in range(n_sel)], axis=0)
    o_sel = attend(k_sel, v_sel, dist, dist >= 0, ksn_ref[0], vsn_ref[0])

    n_buf = kw_ref.shape[0]
    dist_w = n_buf - lax.broadcasted_iota(jnp.int32, (1, n_buf), 1)
    o_win = attend(_pick_group(kw_ref, g, n_groups), _pick_group(vw_ref, g, n_groups), dist_w, dist_w < WINDOW,
                   kwn_ref[0], vwn_ref[0])

    gate = jax.nn.sigmoid(gates_ref[0, 0])
    o = gate[:, 0:1] * ocmp_ref[0, 0] + gate[:, 1:2] * o_sel + gate[:, 2:3] * o_win
    for s in range(gs):
        o_ref[0, :, s * dh:(s + 1) * dh] = o[s:s + 1].astype(o_ref.dtype)


def _nsa_decode_attend(q, gates, o_cmp, nsa_new, win_new, cache, win_state, layer, page_table, top_idx,
                       n_heads, n_groups, pos):
    B, n_pages = page_table.shape
    page = cache.shape[2]
    assert cache.shape[3] == 4
    G = n_groups
    gs = n_heads // G
    dh = HEAD_DIM
    per_page = page // SEL_BLOCK
    n_buf = win_state.shape[2]
    assert page % SEL_BLOCK == 0 and pos >= n_buf
    blk = top_idx[:, :, 0, :SEL_TOPK]
    phys = jnp.take_along_axis(page_table[:, None, :], (blk // per_page).reshape(B, 1, -1), axis=2)
    pool_blk = phys.reshape(B, G, SEL_TOPK) * per_page + blk % per_page
    n_sel = SEL_TOPK

    def cache_spec(j, kind):
        def imap(b, g, pool_blk_ref, blk_ref):
            pb = pool_blk_ref[(b * G + g) * n_sel + j]
            return (layer, pb // per_page, pb % per_page, kind, 0, 0)
        return pl.BlockSpec((None, None, SEL_BLOCK, None, G, dh), imap)

    cache_specs = []
    for j in range(n_sel):
        cache_specs += [cache_spec(j, 2), cache_spec(j, 3)]
    tok = lambda col0: pl.BlockSpec((1, 1, dh), lambda b, g, *_: (b, 0, col0 + g))
    win = lambda kv: pl.BlockSpec((None, None, n_buf, None, G, dh), lambda b, g, *_: (layer, b, 0, kv, 0, 0))
    grid_spec = pltpu.PrefetchScalarGridSpec(
        num_scalar_prefetch=2,
        grid=(B, G),
        in_specs=[pl.BlockSpec(memory_space=pltpu.SMEM),
                  pl.BlockSpec((1, 1, gs * dh), lambda b, g, *_: (b, 0, g)),
                  pl.BlockSpec((1, 1, gs, 3), lambda b, g, *_: (b, g, 0, 0)),
                  pl.BlockSpec((1, 1, gs, dh), lambda b, g, *_: (b, g, 0, 0)),
                  tok(2 * G), tok(3 * G), tok(0), tok(G), win(0), win(1)] + cache_specs,
        out_specs=pl.BlockSpec((1, 1, gs * dh), lambda b, g, *_: (b, 0, g)))
    return pl.pallas_call(
        functools.partial(_nsa_attend_kernel, n_groups=G, gs=gs, pos=pos),
        grid_spec=grid_spec,
        out_shape=jax.ShapeDtypeStruct((B, 1, n_heads * dh), BF16),
        compiler_params=_params("parallel", "arbitrary"),
        name="nsa_decode_attend",
    )(pool_blk.reshape(-1), blk.reshape(-1), _alibi_slopes(n_heads), q, gates, o_cmp,
      nsa_new, nsa_new, win_new, win_new, win_state, win_state, *([cache] * (2 * n_sel)))


def _trunk(x3, start, past, ln_g, ln_b, ffn_gate, ffn_up, ffn_down, attn_w_in, attn_w_out,
           cmp_pe, cmp_w1, cmp_w2, pool_w, pool_scale):
    B, T, D = x3.shape
    M = B * T
    depth = ffn_gate.shape[0]
    d_ff = ffn_gate.shape[-1]
    alpha = (2.0 * depth) ** 0.25
    decode = past is not None
    dh = HEAD_DIM
    n_heads = D // dh
    H_A = n_heads // 2
    NH = n_heads - H_A
    G = cmp_pe.shape[1]
    gs = NH // G
    moba_w, nsa_w, kvw = H_A * dh, NH * dh, G * dh
    tm = min(M, 1024)
    tm_ln = min(M, 256)
    pool_w_mx = pool_w.astype(MXU_DTYPE)
    if decode:
        assert T == 1
        cache_moba_kv, cache_nsa_kv, state_nsa_win, state_pool, page_table = past

    x = x3.reshape(M, D)
    xb = x.astype(BF16)
    new_moba, new_nsa, new_win, new_pool = [], [], [], []

    def ffn(x, xb, layer, sub, ln_idx):
        hidden = _ffn_up(xb, ffn_gate, ffn_up, layer, sub, tm)
        y = _matmul(hidden, ffn_down, (layer, sub), 0, D, F32, tm=tm, tn=256, kchunk=d_ff // 2, name="ffn_down")
        return _res_ln(y, x, ln_g[layer, ln_idx], ln_b[layer, ln_idx], alpha=alpha, coef=0.5, tm=tm_ln)

    for layer in range(depth):
        x, xb = ffn(x, xb, layer, 0, 0)
        if layer % 2 == 0:
            a = layer // 2
            proj = functools.partial(_matmul, xb, attn_w_in, (a,), tm=tm, name="in_proj")
            q_a = proj(0, moba_w, BF16, tn=256)
            moba_kv = proj(moba_w, 2 * moba_w, F32, tn=256)
            q_b = proj(3 * moba_w, nsa_w, BF16, tn=256)
            c0 = 3 * moba_w + nsa_w
            nsa_kv = proj(c0, 4 * kvw, F32, tn=256)
            win_kv = proj(c0 + 4 * kvw, 2 * kvw, F32, tn=256)
            gates = proj(c0 + 6 * kvw, 3 * NH, F32, tn=128)
            moba_new = moba_kv.reshape(B, T, 2, H_A, dh)
            nsa_new = nsa_kv.reshape(B, T, 4, G, dh)
            win_new = win_kv.reshape(B, T, 2, G, dh)
            q_a3, q_b3 = q_a.reshape(B, T, moba_w), q_b.reshape(B, T, nsa_w)
            moba_kv3, nsa_kv3, win_kv3 = (moba_kv.reshape(B, T, 2 * moba_w), nsa_kv.reshape(B, T, 4 * kvw),
                                          win_kv.reshape(B, T, 2 * kvw))
            if not decode:
                o_a = _moba_prompt(q_a3, moba_kv3, H_A)
                kv16 = nsa_new[:, :, 0:2].transpose(0, 2, 3, 1, 4).reshape(B, 2, G, T // 16, 16 * dh)
                cmp_kv = _compress_prompt(kv16, cmp_pe[a].reshape(2, 2, 16 * dh), cmp_w1, cmp_w2, a)
                gates_g = gates.reshape(B, T, 3, G, gs).transpose(0, 3, 1, 2, 4).reshape(B, G, T, 3 * gs)
                o_b = _nsa_prompt(q_b3, gates_g, cmp_kv, nsa_kv3, win_kv3, NH, G)
                win_state = win_new[:, -min(WINDOW, T):]
            else:
                top_a = _moba_decode_select(q_a3, cache_moba_kv, a, page_table, H_A)
                o_a = _moba_decode_attend(q_a3, moba_kv3, cache_moba_kv, a, page_table, top_a, H_A, start)
                cmp_kv = _nsa_decode_compress(cache_nsa_kv, a, page_table, cmp_pe, cmp_w1, cmp_w2, G)
                o_cmp, top_b = _nsa_decode_select(q_b3, cmp_kv, NH, G, start)
                gates_g = gates.reshape(B, 3, G, gs).transpose(0, 2, 3, 1)
                o_b = _nsa_decode_attend(q_b3, gates_g, o_cmp, nsa_kv3, win_kv3, cache_nsa_kv, state_nsa_win, a,
                                         page_table, top_b, NH, G, start)
                win_state = jnp.concatenate([state_nsa_win[a][:, T:], win_new], axis=1)
            mixed = jnp.concatenate([o_a, o_b], axis=-1).reshape(M, D)
            new_moba.append(moba_new)
            new_nsa.append(nsa_new)
            new_win.append(win_state)
            y = _matmul(mixed, attn_w_out, (a,), 0, D, F32, tm=tm, tn=256, name="out_proj")
            x, xb = _res_ln(y, x, ln_g[layer, 1], ln_b[layer, 1], alpha=alpha, coef=1.0, tm=tm_ln)
        else:
            p = layer // 2
            x3c = x.reshape(B, T, D)
            if decode:
                prev = jnp.concatenate([state_pool[p], x3c], axis=1)
                new_pool.append(prev[:, -POOL_BUF:])
                tt = T
            else:
                prev = x3c
                new_pool.append(x3c[:, -POOL_BUF:])
                tt = 256
            xo, xbo = _pool_mixer_ln(x3c, prev, pool_w_mx[p], pool_scale[p], ln_g[layer, 1], ln_b[layer, 1],
                                     alpha=alpha, start=start, tt=tt, decode=decode)
            x, xb = xo.reshape(M, D), xbo.reshape(M, D)
        x, xb = ffn(x, xb, layer, 1, 2)
    return (x.reshape(B, T, D), jnp.stack(new_moba), jnp.stack(new_nsa), jnp.stack(new_win),
            jnp.stack(new_pool))


def kernel(x_prompt, x_sample, cache_moba_kv, cache_nsa_kv, state_nsa_win, state_pool, page_table,
           ln_g, ln_b, ffn_gate, ffn_up, ffn_down, attn_w_in, attn_w_out, cmp_pe, cmp_w1, cmp_w2,
           pool_w, pool_scale):
    weights = (ln_g, ln_b, ffn_gate, ffn_up, ffn_down, attn_w_in, attn_w_out, cmp_pe, cmp_w1, cmp_w2,
               pool_w, pool_scale)
    past_len = page_table.shape[1] * cache_moba_kv.shape[2]
    y_p, moba_p, nsa_p, win_p, pool_p = _trunk(x_prompt, 0, None, *weights)
    y_s, moba_s, nsa_s, win_s, pool_s = _trunk(
        x_sample, past_len, (cache_moba_kv, cache_nsa_kv, state_nsa_win, state_pool, page_table), *weights)
    return (y_p, y_s, moba_p, nsa_p, win_p, pool_p, moba_s, nsa_s, win_s, pool_s)
```

```python
import functools

import jax
import jax.numpy as jnp
import numpy as np
from jax import lax
from jax.experimental import pallas as pl
from jax.experimental.pallas import tpu as pltpu

F32 = jnp.float32
BF16 = jnp.bfloat16
MXU_DTYPE = BF16

HEAD_DIM = 128
MOBA_BLOCK = 256
MOBA_TOPK = 3
CMP_LEN = 32
CMP_STRIDE = 16
SEL_BLOCK = 64
SEL_TOPK = 15
WINDOW = 512
POOL_WINDOWS = (2, 4, 8, 16)
POOL_BUF = max(POOL_WINDOWS) - 1
LN_EPS = 1e-5
SCALE = HEAD_DIM ** -0.5
NEG = -1e30
LOWEST = -3e38
NSA_Q_TILE = 256
NSA_KEY_CHUNK = 1024
MOBA_KEY_CHUNK = 1024
MOBA_HEADS_PER_STEP = 2
SUBLANES = 8
DECODE_PAGES_PER_STEP = 4

V7X_VMEM_BYTES = 64 * 1024 * 1024
VMEM_LIMIT = V7X_VMEM_BYTES - 8 * 1024 * 1024


def _params(*sem):
    return pltpu.CompilerParams(dimension_semantics=sem, vmem_limit_bytes=VMEM_LIMIT)


def _mx(a):
    return a.astype(MXU_DTYPE)


def _dot(a, b):
    return jnp.dot(_mx(a), _mx(b), preferred_element_type=F32)


def _dot_tn(a, b):
    return lax.dot_general(_mx(a), _mx(b), (((0,), (0,)), ((), ())), preferred_element_type=F32)


def _dot_nt(a, b):
    return lax.dot_general(_mx(a), _mx(b), (((1,), (1,)), ((), ())), preferred_element_type=F32)


def _alibi_slopes(n):
    return jnp.asarray(np.exp2(-8.0 * np.arange(1, n + 1) / n), dtype=F32)


def _layernorm_rows(z, g, b):
    mu = jnp.mean(z, axis=-1, keepdims=True)
    zc = z - mu
    var = jnp.mean(zc * zc, axis=-1, keepdims=True)
    return zc * lax.rsqrt(var + LN_EPS) * g + b


def _ffn_up_kernel(x_ref, wg_ref, wu_ref, o_ref):
    x = x_ref[...]
    g = _dot(x, wg_ref[...])
    u = _dot(x, wu_ref[...])
    o_ref[...] = (g * jax.nn.sigmoid(g) * u).astype(o_ref.dtype)


def _ffn_up(xb, w_gate, w_up, layer, sub, tm, tn=256):
    M, D = xb.shape
    F = w_gate.shape[-1]
    assert M % tm == 0 and F % tn == 0
    wspec = pl.BlockSpec((None, None, D, tn), lambda i, j: (layer, sub, 0, j))
    return pl.pallas_call(
        _ffn_up_kernel,
        grid=(M // tm, F // tn),
        in_specs=[pl.BlockSpec((tm, D), lambda i, j: (i, 0)), wspec, wspec],
        out_specs=pl.BlockSpec((tm, tn), lambda i, j: (i, j)),
        out_shape=jax.ShapeDtypeStruct((M, F), BF16),
        compiler_params=_params("parallel", "arbitrary"),
        name="ffn_up",
    )(xb, w_gate, w_up)


def _matmul_kernel(x_ref, w_ref, o_ref, *, kchunk):
    K = x_ref.shape[1]
    acc = None
    for k0 in range(0, K, kchunk):
        part = _dot(x_ref[:, k0:k0 + kchunk], w_ref[k0:k0 + kchunk, :])
        acc = part if acc is None else acc + part
    o_ref[...] = acc[:, :o_ref.shape[1]].astype(o_ref.dtype)


def _matmul(xb, w, widx, col0, ncols, out_dtype, *, tm, tn, kchunk=None, name):
    M, K = xb.shape
    assert M % tm == 0 and col0 % tn == 0
    kchunk = K if kchunk is None else kchunk
    assert K % kchunk == 0
    if ncols % tn == 0:
        nj, out_tn = ncols // tn, tn
    else:
        assert ncols < tn
        nj, out_tn = 1, ncols
    j0 = col0 // tn
    lead = (None,) * len(widx)
    x_spec = pl.BlockSpec((tm, K), lambda i, j: (i, 0), pipeline_mode=pl.Buffered(1))
    return pl.pallas_call(
        functools.partial(_matmul_kernel, kchunk=kchunk),
        grid=(M // tm, nj),
        in_specs=[x_spec, pl.BlockSpec(lead + (K, tn), lambda i, j: tuple(widx) + (0, j0 + j))],
        out_specs=pl.BlockSpec((tm, out_tn), lambda i, j: (i, j)),
        out_shape=jax.ShapeDtypeStruct((M, ncols), out_dtype),
        compiler_params=_params("parallel", "arbitrary"),
        name=name,
    )(xb, w)


def _res_ln_kernel(y_ref, res_ref, g_ref, b_ref, o_ref, ob_ref, *, alpha, coef):
    z = alpha * res_ref[...] + coef * y_ref[...]
    y = _layernorm_rows(z, g_ref[...], b_ref[...])
    o_ref[...] = y
    ob_ref[...] = y.astype(ob_ref.dtype)


def _res_ln(y, res, g, b, *, alpha, coef, tm):
    M, D = y.shape
    assert M % tm == 0
    row = pl.BlockSpec((1, D), lambda i: (0, 0))
    tile = pl.BlockSpec((tm, D), lambda i: (i, 0))
    return pl.pallas_call(
        functools.partial(_res_ln_kernel, alpha=alpha, coef=coef),
        grid=(M // tm,),
        in_specs=[tile, tile, row, row],
        out_specs=[tile, tile],
        out_shape=[jax.ShapeDtypeStruct((M, D), F32), jax.ShapeDtypeStruct((M, D), BF16)],
        compiler_params=_params("parallel"),
        name="res_ln",
    )(y, res, g.reshape(1, D), b.reshape(1, D))


def _pool_kernel(x_ref, prev_ref, w_ref, sc_ref, g_ref, b_ref, o_ref, ob_ref, *,
                 alpha, start, tt, decode, group):
    ti = pl.program_id(1)
    zs = []
    for gi, win in enumerate(POOL_WINDOWS):
        c0, c1 = gi * group, (gi + 1) * group
        prev = prev_ref[0, :, c0:c1]
        if decode:
            xg = prev[POOL_BUF:POOL_BUF + 1]
            wsum = jnp.sum(prev[POOL_BUF + 1 - win:], axis=0, keepdims=True)
            cnt = jnp.full((1, 1), float(min(win, start + 1)), F32)
        else:
            xg = x_ref[0, :, c0:c1]
            prev = jnp.where(ti == 0, 0.0, prev)
            s = jnp.concatenate([prev, xg], axis=0)
            span = 1
            while span < win:
                s = s + pltpu.roll(s, span, axis=0)
                span *= 2
            wsum = s[POOL_BUF + 1:]
            pos = start + ti * tt + lax.broadcasted_iota(jnp.int32, (tt, 1), 0)
            cnt = jnp.minimum(win, pos + 1).astype(F32)
        pooled = wsum / cnt - xg
        y = _dot(pooled, w_ref[gi]) * sc_ref[:, c0:c1]
        zs.append(alpha * xg + y)
    d = group * len(POOL_WINDOWS)
    mu = sum(jnp.sum(z, axis=-1, keepdims=True) for z in zs) / d
    var = sum(jnp.sum((z - mu) * (z - mu), axis=-1, keepdims=True) for z in zs) / d
    inv = lax.rsqrt(var + LN_EPS)
    for gi, z in enumerate(zs):
        c0, c1 = gi * group, (gi + 1) * group
        y = (z - mu) * inv * g_ref[:, c0:c1] + b_ref[:, c0:c1]
        o_ref[0, :, c0:c1] = y
        ob_ref[0, :, c0:c1] = y.astype(ob_ref.dtype)


def _pool_mixer_ln(x3, prev3, w_pool, scale, g, b, *, alpha, start, tt, decode):
    B, T, D = x3.shape
    group = D // len(POOL_WINDOWS)
    assert T % tt == 0 and (decode or tt % 16 == 0)
    per = tt // 16 if not decode else 0
    if decode:
        prev_map = lambda bi, ti: (bi, 0, 0)
    else:
        prev_map = lambda bi, ti: (bi, jnp.maximum(ti * per - 1, 0), 0)
    row = pl.BlockSpec((1, D), lambda bi, ti: (0, 0))
    tile = pl.BlockSpec((1, tt, D), lambda bi, ti: (bi, ti, 0))
    return pl.pallas_call(
        functools.partial(_pool_kernel, alpha=alpha, start=start, tt=tt, decode=decode, group=group),
        grid=(B, T // tt),
        in_specs=[tile, pl.BlockSpec((1, 16, D), prev_map),
                  pl.BlockSpec(w_pool.shape, lambda bi, ti: (0, 0, 0)), row, row, row],
        out_specs=[tile, tile],
        out_shape=[jax.ShapeDtypeStruct((B, T, D), F32), jax.ShapeDtypeStruct((B, T, D), BF16)],
        compiler_params=_params("parallel", "arbitrary"),
        name="pool_mixer_ln",
    )(x3, prev3, w_pool, scale.reshape(1, D), g.reshape(1, D), b.reshape(1, D))


def _topk_mask_t(score_t, row, n_valid, k):
    n = score_t.shape[0]
    masked = jnp.where(row < n_valid, score_t, NEG)
    rank = jnp.zeros(score_t.shape, jnp.int32)
    for j in range(n):
        other = masked[j:j + 1, :]
        ahead = (other > masked) | ((other == masked) & (j < row))
        rank = rank + ahead.astype(jnp.int32)
    return (rank < k) & (row < n_valid)


def _online_update(m_ref, l_ref, acc_ref, s, v):
    m_old = m_ref[...]
    m_new = jnp.maximum(m_old, jnp.max(s, axis=-1, keepdims=True))
    p = jnp.exp(s - m_new)
    a = jnp.exp(m_old - m_new)
    l_ref[...] = a * l_ref[...] + jnp.sum(p, axis=-1, keepdims=True)
    acc_ref[...] = a * acc_ref[...] + _dot(p, v)
    m_ref[...] = m_new


def _moba_kernel(slopes_ref, q_ref, k_ref, v_ref, ex_ref, o_ref, m_ref, l_ref, acc_ref, kmean_ref, *, nb, hpg):
    hg = pl.program_id(1)
    qi = pl.program_id(2)
    blk = MOBA_BLOCK
    dh = HEAD_DIM
    T = nb * blk
    kc = min(T, MOBA_KEY_CHUNK)
    heads = [(e * dh, (e + 1) * dh) for e in range(hpg)]
    slopes = [slopes_ref[hg * hpg + e] for e in range(hpg)]
    qs = [q_ref[0, :, c0:c1] for c0, c1 in heads]
    row = lax.broadcasted_iota(jnp.int32, (nb, blk), 0)

    @pl.when(qi == 0)
    def _():
        for e, (c0, c1) in enumerate(heads):
            kmean_ref[e] = jnp.concatenate(
                [jnp.sum(k_ref[0, n * blk:(n + 1) * blk, c0:c1], axis=0, keepdims=True) for n in range(nb)],
                axis=0) * (1.0 / blk)

    sel_keys = []
    for e, (c0, c1) in enumerate(heads):
        gate_t = _dot_nt(kmean_ref[e], qs[e])
        sel_t = _topk_mask_t(gate_t, row, qi, MOBA_TOPK) | (row == qi)
        sel_keys.append(_dot_tn(sel_t.astype(F32), ex_ref[...]))
    qpos = qi * blk + lax.broadcasted_iota(jnp.int32, (blk, 1), 0)

    m_ref[...] = jnp.full(m_ref.shape, NEG, F32)
    l_ref[...] = jnp.zeros(l_ref.shape, F32)
    acc_ref[...] = jnp.zeros(acc_ref.shape, F32)
    for c in reversed(range(T // kc)):
        @pl.when(c * kc <= qi * blk)
        def _():
            dist = qpos - (c * kc + lax.broadcasted_iota(jnp.int32, (1, kc), 1))
            causal = dist >= 0
            distf = dist.astype(F32)
            for e, (c0, c1) in enumerate(heads):
                ok = (sel_keys[e][:, c * kc:(c + 1) * kc] > 0.5) & causal
                s = (_dot_nt(qs[e], k_ref[0, c * kc:(c + 1) * kc, c0:c1]) * SCALE - slopes[e] * distf
                     + jnp.where(ok, 0.0, NEG))
                _online_update(m_ref.at[e], l_ref.at[e], acc_ref.at[e], s, v_ref[0, c * kc:(c + 1) * kc, c0:c1])
    for e, (c0, c1) in enumerate(heads):
        o_ref[0, :, c0:c1] = (acc_ref[e] / l_ref[e]).astype(o_ref.dtype)


def _moba_prompt(q, kv, n_heads):
    B, T, _ = q.shape
    hpg = MOBA_HEADS_PER_STEP
    assert T % MOBA_BLOCK == 0 and T % min(T, MOBA_KEY_CHUNK) == 0 and MOBA_KEY_CHUNK % MOBA_BLOCK == 0
    assert n_heads % hpg == 0
    nb = T // MOBA_BLOCK
    dh = HEAD_DIM
    w = hpg * dh
    expand = (np.arange(T)[None, :] // MOBA_BLOCK) == np.arange(nb)[:, None]
    expand = jnp.asarray(expand.astype(np.float32), BF16)
    return pl.pallas_call(
        functools.partial(_moba_kernel, nb=nb, hpg=hpg),
        grid=(B, n_heads // hpg, nb),
        in_specs=[pl.BlockSpec(memory_space=pltpu.SMEM),
                  pl.BlockSpec((1, MOBA_BLOCK, w), lambda b, h, i: (b, i, h)),
                  pl.BlockSpec((1, T, w), lambda b, h, i: (b, 0, h)),
                  pl.BlockSpec((1, T, w), lambda b, h, i: (b, 0, n_heads // hpg + h)),
                  pl.BlockSpec((nb, T), lambda b, h, i: (0, 0))],
        out_specs=pl.BlockSpec((1, MOBA_BLOCK, w), lambda b, h, i: (b, i, h)),
        out_shape=jax.ShapeDtypeStruct((B, T, n_heads * dh), BF16),
        scratch_shapes=[pltpu.VMEM((hpg, MOBA_BLOCK, 1), F32), pltpu.VMEM((hpg, MOBA_BLOCK, 1), F32),
                        pltpu.VMEM((hpg, MOBA_BLOCK, dh), F32), pltpu.VMEM((hpg, nb, dh), F32)],
        compiler_params=_params("parallel", "parallel", "arbitrary"),
        name="moba_prompt",
    )(_alibi_slopes(n_heads), q, kv, kv, expand)


def _compress_kernel(x_ref, pe_ref, w1_ref, w2_ref, o_ref, *, nc):
    x = x_ref[0, 0, 0]
    half = x.shape[1]
    lo = _dot(x + pe_ref[0, 0:1, :], w1_ref[:half, :])
    hi = _dot(x + pe_ref[0, 1:2, :], w1_ref[half:, :])
    rows = x.shape[0]
    hid = lo + pltpu.roll(hi, rows - 1, axis=0)
    y = _dot(hid * jax.nn.sigmoid(hid), w2_ref[...])
    ridx = lax.broadcasted_iota(jnp.int32, y.shape, 0)
    o_ref[0, 0, 0] = jnp.where(ridx < nc, y, 0.0)


def _compress_prompt(kv16, pe, w1, w2, layer):
    B, two, G, R, W = kv16.shape
    nc = R - 1
    hidden = w1.shape[-1]
    return pl.pallas_call(
        functools.partial(_compress_kernel, nc=nc),
        grid=(B, two, G),
        in_specs=[pl.BlockSpec((1, 1, 1, R, W), lambda b, c, g: (b, c, g, 0, 0)),
                  pl.BlockSpec((1, 2, W), lambda b, c, g: (c, 0, 0)),
                  pl.BlockSpec((None, None, 2 * W, hidden), lambda b, c, g: (layer, c, 0, 0)),
                  pl.BlockSpec((None, None, hidden, HEAD_DIM), lambda b, c, g: (layer, c, 0, 0))],
        out_specs=pl.BlockSpec((1, 1, 1, R, HEAD_DIM), lambda b, c, g: (b, c, g, 0, 0)),
        out_shape=jax.ShapeDtypeStruct((B, two, G, R, HEAD_DIM), F32),
        compiler_params=_params("parallel", "parallel", "parallel"),
        name="nsa_compress",
    )(kv16, pe, w1, w2)


def _nsa_kernel(slopes_ref, q_ref, gates_ref, kc_ref, vc_ref, ks_ref, vs_ref, kw_ref, vw_ref,
                ov_ref, ex_ref, o_ref, m_ref, l_ref, acc_ref, *, T, gs):
    g = pl.program_id(1)
    qi = pl.program_id(2)
    tq = NSA_Q_TILE
    dh = HEAD_DIM
    q = q_ref[0]
    qs = jnp.concatenate([q[:, s * dh:(s + 1) * dh] for s in range(gs)], axis=0)
    slopes = [slopes_ref[g * gs + s] for s in range(gs)]
    heads = [(s * tq, (s + 1) * tq) for s in range(gs)]
    tpos = qi * tq + lax.broadcasted_iota(jnp.int32, (tq, 1), 0)

    ncp = kc_ref.shape[3]
    c_end = lax.broadcasted_iota(jnp.int32, (1, ncp), 1) * CMP_STRIDE + (CMP_LEN - 1)
    dist_c = tpos - c_end
    cmask = dist_c >= 0
    bias_c = jnp.where(cmask, 0.0, NEG)
    distf_c = dist_c.astype(F32)
    s_all = _dot_nt(qs, kc_ref[0, 0, 0])
    p_list = []
    for s, (r0, r1) in enumerate(heads):
        sc = s_all[r0:r1] * SCALE - slopes[s] * distf_c + bias_c
        e = jnp.where(cmask, jnp.exp(sc - jnp.max(sc, axis=-1, keepdims=True)), 0.0)
        den = jnp.sum(e, axis=-1, keepdims=True)
        p_list.append(e / jnp.where(den > 0.0, den, 1.0))
    p_c = _mx(jnp.concatenate(p_list, axis=0))
    o_cmp = _dot(p_c, vc_ref[0, 0, 0])

    imp_cols = _dot_nt(ov_ref[...], p_c)
    imp_t = imp_cols[:, 0:tq]
    for r0, r1 in heads[1:]:
        imp_t = imp_t + imp_cols[:, r0:r1]
    ns = imp_t.shape[0]
    rowj = lax.broadcasted_iota(jnp.int32, (ns, tq), 0)
    cur_t = (qi * tq + lax.broadcasted_iota(jnp.int32, (1, tq), 1)) // SEL_BLOCK
    sel_t = _topk_mask_t(imp_t, rowj, cur_t, SEL_TOPK).astype(F32)
    sel_keys = _dot_tn(sel_t, ex_ref[...])

    m_ref[...] = jnp.full(m_ref.shape, NEG, F32)
    l_ref[...] = jnp.zeros(l_ref.shape, F32)
    acc_ref[...] = jnp.zeros(acc_ref.shape, F32)
    kc = NSA_KEY_CHUNK
    for c in reversed(range(T // kc)):
        @pl.when(c * kc < (qi + 1) * tq)
        def _():
            kpos = c * kc + lax.broadcasted_iota(jnp.int32, (1, kc), 1)
            dist = tpos - kpos
            own = (kpos // SEL_BLOCK) == (tpos // SEL_BLOCK)
            ok = ((sel_keys[:, c * kc:(c + 1) * kc] > 0.5) | own) & (dist >= 0)
            bias = jnp.where(ok, 0.0, NEG)
            distf = dist.astype(F32)
            s_all = _dot_nt(qs, ks_ref[0, c * kc:(c + 1) * kc, :])
            m_all = m_ref[...]
            ps, maxes, sums = [], [], []
            for s, (r0, r1) in enumerate(heads):
                sc = s_all[r0:r1] * SCALE - slopes[s] * distf + bias
                m_new = jnp.maximum(m_all[r0:r1], jnp.max(sc, axis=-1, keepdims=True))
                p = jnp.exp(sc - m_new)
                maxes.append(m_new)
                sums.append(jnp.sum(p, axis=-1, keepdims=True))
                ps.append(_mx(p))
            m_new = jnp.concatenate(maxes, axis=0)
            a = jnp.exp(m_all - m_new)
            pv = _dot(jnp.concatenate(ps, axis=0), vs_ref[0, c * kc:(c + 1) * kc, :])
            l_ref[...] = a * l_ref[...] + jnp.concatenate(sums, axis=0)
            acc_ref[...] = a * acc_ref[...] + pv
            m_ref[...] = m_new
    o_sel = acc_ref[...] / l_ref[...]

    span = WINDOW + tq
    w0 = pl.multiple_of(jnp.maximum(qi * tq - WINDOW, 0), tq)
    kpos = w0 + lax.broadcasted_iota(jnp.int32, (1, span), 1)
    dist = tpos - kpos
    bias = jnp.where((dist >= 0) & (dist < WINDOW), 0.0, NEG)
    distf = dist.astype(F32)
    s_all = _dot_nt(qs, kw_ref[0, pl.ds(w0, span), :])
    ps = []
    for s, (r0, r1) in enumerate(heads):
        sc = s_all[r0:r1] * SCALE - slopes[s] * distf + bias
        e = jnp.exp(sc - jnp.max(sc, axis=-1, keepdims=True))
        ps.append(_mx(e / jnp.sum(e, axis=-1, keepdims=True)))
    o_win = _dot(jnp.concatenate(ps, axis=0), vw_ref[0, pl.ds(w0, span), :])

    gate = jax.nn.sigmoid(gates_ref[0, 0])
    for s, (r0, r1) in enumerate(heads):
        o = (gate[:, s:s + 1] * o_cmp[r0:r1] + gate[:, gs + s:gs + s + 1] * o_sel[r0:r1]
             + gate[:, 2 * gs + s:2 * gs + s + 1] * o_win[r0:r1])
        o_ref[0, :, s * dh:(s + 1) * dh] = o.astype(o_ref.dtype)


def _nsa_tables(T, rows_cmp):
    ns = -(-T // SEL_BLOCK)
    c_start = np.arange(rows_cmp) * CMP_STRIDE
    s_start = np.arange(ns) * SEL_BLOCK
    nc = (T - CMP_LEN) // CMP_STRIDE + 1
    overlap = ((c_start[:, None] <= s_start[None, :] + SEL_BLOCK - 1)
               & (c_start[:, None] + CMP_LEN - 1 >= s_start[None, :])
               & (np.arange(rows_cmp)[:, None] < nc))
    expand = (np.arange(T)[None, :] // SEL_BLOCK) == np.arange(ns)[:, None]
    return jnp.asarray(overlap.T.astype(np.float32), BF16), jnp.asarray(expand.astype(np.float32), BF16)


def _nsa_prompt(q, gates, cmp_kv, nsa_kv, win_kv, n_heads, n_groups):
    B, T, _ = q.shape
    G = n_groups
    gs = n_heads // G
    dh = HEAD_DIM
    tq = NSA_Q_TILE
    assert T % NSA_KEY_CHUNK == 0 and T >= WINDOW + tq and T % tq == 0
    R = cmp_kv.shape[3]
    overlap_t, expand = _nsa_tables(T, R)
    ns = overlap_t.shape[0]
    rows = gs * tq
    seq = lambda col: pl.BlockSpec((1, T, dh), lambda b, g, i: (b, 0, col(g)))
    return pl.pallas_call(
        functools.partial(_nsa_kernel, T=T, gs=gs),
        grid=(B, G, T // tq),
        in_specs=[pl.BlockSpec(memory_space=pltpu.SMEM),
                  pl.BlockSpec((1, tq, gs * dh), lambda b, g, i: (b, i, g)),
                  pl.BlockSpec((1, 1, tq, 3 * gs), lambda b, g, i: (b, g, i, 0)),
                  pl.BlockSpec((1, 1, 1, R, dh), lambda b, g, i: (b, 0, g, 0, 0)),
                  pl.BlockSpec((1, 1, 1, R, dh), lambda b, g, i: (b, 1, g, 0, 0)),
                  seq(lambda g: 2 * G + g), seq(lambda g: 3 * G + g),
                  seq(lambda g: g), seq(lambda g: G + g),
                  pl.BlockSpec((ns, R), lambda b, g, i: (0, 0)),
                  pl.BlockSpec((ns, T), lambda b, g, i: (0, 0))],
        out_specs=pl.BlockSpec((1, tq, gs * dh), lambda b, g, i: (b, i, g)),
        out_shape=jax.ShapeDtypeStruct((B, T, n_heads * dh), BF16),
        scratch_shapes=[pltpu.VMEM((rows, 1), F32), pltpu.VMEM((rows, 1), F32), pltpu.VMEM((rows, dh), F32)],
        compiler_params=_params("parallel", "parallel", "arbitrary"),
        name="nsa_prompt",
    )(_alibi_slopes(n_heads), q, gates, cmp_kv, cmp_kv, nsa_kv, nsa_kv, win_kv, win_kv, overlap_t, expand)


def _first_argmax(score, lane):
    mx = jnp.max(score, axis=-1, keepdims=True)
    return jnp.min(jnp.where(score == mx, lane, float(score.shape[1])), axis=-1, keepdims=True)


def _topk_indices(score, k, width):
    lane = lax.broadcasted_iota(jnp.int32, score.shape, 1).astype(F32)
    out_lane = lax.broadcasted_iota(jnp.int32, (score.shape[0], width), 1)
    out = jnp.zeros((score.shape[0], width), jnp.int32)
    for j in range(k):
        idx = _first_argmax(score, lane)
        out = jnp.where(out_lane == j, idx.astype(jnp.int32), out)
        score = jnp.where(lane == idx, LOWEST, score)
    return out


def _moba_select_kernel(pt_ref, q_ref, *rest, n_heads, n_steps, pages_per_block):
    cache_refs, (idx_ref, ksum_ref) = rest[:-2], rest[-2:]
    st = pl.program_id(1)
    blocks_per_step = len(cache_refs) // pages_per_block
    for i in range(blocks_per_step):
        total = None
        for r in range(pages_per_block):
            part = jnp.sum(cache_refs[i * pages_per_block + r][...], axis=0, keepdims=True)
            total = part if total is None else total + part
        ksum_ref[pl.ds(st * blocks_per_step + i, 1)] = total

    @pl.when(st == n_steps - 1)
    def _():
        dh = HEAD_DIM
        q = q_ref[0]
        gate = jnp.concatenate(
            [_dot_nt(q[:, h * dh:(h + 1) * dh], ksum_ref[:, h, :] * (1.0 / MOBA_BLOCK)) for h in range(n_heads)],
            axis=0)
        idx_ref[0] = _topk_indices(gate, MOBA_TOPK, idx_ref.shape[2])


def _moba_decode_select(q, cache, layer, page_table, n_heads):
    B, n_pages = page_table.shape
    page = cache.shape[2]
    width = n_heads * HEAD_DIM
    ppb = MOBA_BLOCK // page
    pps = DECODE_PAGES_PER_STEP
    assert MOBA_BLOCK % page == 0 and pps % ppb == 0 and n_pages % pps == 0 and n_pages // ppb >= MOBA_TOPK
    n_steps = n_pages // pps

    def cache_spec(i):
        return pl.BlockSpec((None, None, page, None, n_heads, HEAD_DIM),
                            lambda b, st, pt: (layer, pt[b * n_pages + st * pps + i], 0, 0, 0, 0))

    grid_spec = pltpu.PrefetchScalarGridSpec(
        num_scalar_prefetch=1,
        grid=(B, n_steps),
        in_specs=[pl.BlockSpec((1, 1, width), lambda b, st, pt: (b, 0, 0))] + [cache_spec(i) for i in range(pps)],
        out_specs=pl.BlockSpec((1, n_heads, 128), lambda b, st, pt: (b, 0, 0)),
        scratch_shapes=[pltpu.VMEM((n_pages // ppb, n_heads, HEAD_DIM), F32)])
    return pl.pallas_call(
        functools.partial(_moba_select_kernel, n_heads=n_heads, n_steps=n_steps, pages_per_block=ppb),
        grid_spec=grid_spec,
        out_shape=jax.ShapeDtypeStruct((B, n_heads, 128), jnp.int32),
        compiler_params=_params("parallel", "arbitrary"),
        name="moba_decode_select",
    )(page_table.reshape(-1), q, *([cache] * pps))


def _moba_attend_kernel(phys_ref, blk_ref, slopes_ref, q_ref, kn_ref, vn_ref, *rest, n_heads, pos, page, ppb):
    cache_refs, o_ref = rest[:-1], rest[-1]
    b = pl.program_id(0)
    h = pl.program_id(1)
    slope = slopes_ref[h]
    q = q_ref[0]
    n_sel = len(cache_refs) // (2 * ppb)
    rows = page * SUBLANES
    ridx = lax.broadcasted_iota(jnp.int32, (1, rows), 1)
    mine = (ridx % SUBLANES) == (h % SUBLANES)
    scores, values = [], []
    for j in range(n_sel):
        blk = blk_ref[(b * n_heads + h) * n_sel + j]
        for r in range(ppb):
            k_ref, v_ref = cache_refs[2 * (j * ppb + r)], cache_refs[2 * (j * ppb + r) + 1]
            dist = pos - (blk * MOBA_BLOCK + r * page + ridx // SUBLANES)
            s = _dot_nt(q, k_ref[...].reshape(rows, HEAD_DIM)) * SCALE - slope * dist.astype(F32)
            scores.append(jnp.where(mine, s, NEG))
            values.append(v_ref)
    s_new = jnp.sum(_mx(q).astype(F32) * _mx(kn_ref[0]).astype(F32), axis=-1, keepdims=True) * SCALE
    m = s_new
    for s in scores:
        m = jnp.maximum(m, jnp.max(s, axis=-1, keepdims=True))
    p_new = jnp.exp(s_new - m)
    den = p_new
    acc = _mx(p_new).astype(F32) * _mx(vn_ref[0]).astype(F32)
    for s, v_ref in zip(scores, values):
        p = jnp.exp(s - m)
        den = den + jnp.sum(p, axis=-1, keepdims=True)
        acc = acc + _dot(p, v_ref[...].reshape(rows, HEAD_DIM))
    o_ref[0] = (acc / den).astype(o_ref.dtype)


def _moba_decode_attend(q, kv_new, cache, layer, page_table, top_idx, n_heads, pos):
    B, n_pages = page_table.shape
    page = cache.shape[2]
    dh = HEAD_DIM
    ppb = MOBA_BLOCK // page
    blk = top_idx[:, :, :MOBA_TOPK]
    pages = blk[..., None] * ppb + jnp.arange(ppb, dtype=jnp.int32)
    phys = jnp.take_along_axis(page_table[:, None, :], pages.reshape(B, 1, -1), axis=2)
    n_sel = MOBA_TOPK

    assert n_heads % SUBLANES == 0
    hb = n_heads // SUBLANES
    cache = cache.reshape(cache.shape[:3] + (2 * hb, SUBLANES, dh))

    def cache_spec(j, r, kv):
        def imap(b, h, phys_ref, blk_ref):
            return (layer, phys_ref[((b * n_heads + h) * n_sel + j) * ppb + r], 0, kv * hb + h // SUBLANES, 0, 0)
        return pl.BlockSpec((None, None, page, None, SUBLANES, dh), imap)

    cache_specs = []
    for j in range(n_sel):
        for r in range(ppb):
            cache_specs += [cache_spec(j, r, 0), cache_spec(j, r, 1)]
    tok = lambda col0: pl.BlockSpec((1, 1, dh), lambda b, h, *_: (b, 0, col0 + h))
    grid_spec = pltpu.PrefetchScalarGridSpec(
        num_scalar_prefetch=2,
        grid=(B, n_heads),
        in_specs=[pl.BlockSpec(memory_space=pltpu.SMEM), tok(0), tok(0), tok(n_heads)] + cache_specs,
        out_specs=tok(0))
    return pl.pallas_call(
        functools.partial(_moba_attend_kernel, n_heads=n_heads, pos=pos, page=page, ppb=ppb),
        grid_spec=grid_spec,
        out_shape=jax.ShapeDtypeStruct((B, 1, n_heads * dh), BF16),
        compiler_params=_params("parallel", "arbitrary"),
        name="moba_decode_attend",
    )(phys.reshape(-1), blk.reshape(-1), _alibi_slopes(n_heads), q, kv_new, kv_new,
      *([cache] * (2 * n_sel * ppb)))


def _nsa_decode_compress_kernel(pt_ref, *refs, n_steps, page, n_groups):
    n_in = len(refs) - 5
    cache_refs = refs[:n_in]
    pe_ref, w1_ref, w2_ref, o_ref, buf_ref = refs[n_in:]
    st = pl.program_id(1)
    pps = n_in // 2
    for i in range(pps):
        row0 = pl.multiple_of((st * pps + i) * page, page)
        for c in range(2 * n_groups):
            buf_ref[c, pl.ds(row0, page), :] = cache_refs[2 * i + c // n_groups][:, c % n_groups, :]

    @pl.when(st == n_steps - 1)
    def _():
        dh = HEAD_DIM
        rows = buf_ref.shape[1] // CMP_STRIDE
        halves = CMP_LEN // CMP_STRIDE
        for c in range(2 * n_groups):
            which = c // n_groups
            parts = [None] * halves
            for l in range(CMP_STRIDE):
                x = buf_ref.at[c][pl.ds(l, rows, stride=CMP_STRIDE), :]
                for hf in range(halves):
                    ll = hf * CMP_STRIDE + l
                    d = _dot(x + pe_ref[which, ll:ll + 1, :], w1_ref[which, ll * dh:(ll + 1) * dh, :])
                    parts[hf] = d if parts[hf] is None else parts[hf] + d
            hid = parts[0]
            for hf in range(1, halves):
                hid = hid + pltpu.roll(parts[hf], rows - hf, axis=0)
            y = _dot(hid * jax.nn.sigmoid(hid), w2_ref[which])
            ridx = lax.broadcasted_iota(jnp.int32, y.shape, 0)
            o_ref[0, c] = jnp.where(ridx < rows - (halves - 1), y, 0.0)


def _nsa_decode_compress(cache, layer, page_table, pe, w1, w2, n_groups):
    B, n_pages = page_table.shape
    page = cache.shape[2]
    dh = HEAD_DIM
    past = n_pages * page
    rows = past // CMP_STRIDE
    hidden = w1.shape[-1]

    pps = DECODE_PAGES_PER_STEP
    assert n_pages % pps == 0
    n_steps = n_pages // pps

    def cache_spec(i, which):
        return pl.BlockSpec((None, None, page, None, n_groups, dh),
                            lambda b, st, pt: (layer, pt[b * n_pages + st * pps + i], 0, which, 0, 0))

    cache_specs = [cache_spec(i, which) for i in range(pps) for which in range(2)]
    grid_spec = pltpu.PrefetchScalarGridSpec(
        num_scalar_prefetch=1,
        grid=(B, n_steps),
        in_specs=cache_specs + [
                  pl.BlockSpec((None, 2, CMP_LEN, dh), lambda b, p, pt: (layer, 0, 0, 0)),
                  pl.BlockSpec((None, 2, CMP_LEN * dh, hidden), lambda b, p, pt: (layer, 0, 0, 0)),
                  pl.BlockSpec((None, 2, hidden, dh), lambda b, p, pt: (layer, 0, 0, 0))],
        out_specs=pl.BlockSpec((1, 2 * n_groups, rows, dh), lambda b, p, pt: (b, 0, 0, 0)),
        scratch_shapes=[pltpu.VMEM((2 * n_groups, past, dh), F32)])
    return pl.pallas_call(
        functools.partial(_nsa_decode_compress_kernel, n_steps=n_steps, page=page, n_groups=n_groups),
        grid_spec=grid_spec,
        out_shape=jax.ShapeDtypeStruct((B, 2 * n_groups, rows, dh), F32),
        compiler_params=_params("parallel", "arbitrary"),
        name="nsa_decode_compress",
    )(page_table.reshape(-1), *([cache] * (2 * pps)), pe, w1, w2)


def _split_heads(q, gs):
    dh = HEAD_DIM
    return jnp.concatenate([q[:, s * dh:(s + 1) * dh] for s in range(gs)], axis=0)


def _group_slopes(slopes_ref, g, gs):
    sub = lax.broadcasted_iota(jnp.int32, (gs, 1), 0)
    slope = jnp.zeros((gs, 1), F32)
    for s in range(gs):
        slope = jnp.where(sub == s, slopes_ref[g * gs + s], slope)
    return slope


def _nsa_select_kernel(slopes_ref, q_ref, kc_ref, vc_ref, ov_ref, ocmp_ref, idx_ref, *, gs, pos):
    g = pl.program_id(1)
    qs = _split_heads(q_ref[0], gs)
    slope = _group_slopes(slopes_ref, g, gs)
    ncp = kc_ref.shape[2]
    c_end = lax.broadcasted_iota(jnp.int32, (1, ncp), 1) * CMP_STRIDE + (CMP_LEN - 1)
    dist_c = pos - c_end
    cmask = dist_c >= 0
    s_c = jnp.where(cmask, _dot_nt(qs, kc_ref[0, 0]) * SCALE - slope * dist_c.astype(F32), NEG)
    e_c = jnp.where(cmask, jnp.exp(s_c - jnp.max(s_c, axis=-1, keepdims=True)), 0.0)
    den = jnp.sum(e_c, axis=-1, keepdims=True)
    p_c = e_c / jnp.where(den > 0.0, den, 1.0)
    ocmp_ref[0, 0] = _dot(p_c, vc_ref[0, 0])
    imp = jnp.sum(_dot(p_c, ov_ref[...]), axis=0, keepdims=True)
    idx = _topk_indices(imp, SEL_TOPK, idx_ref.shape[3])
    idx_ref[0, 0] = jnp.broadcast_to(idx, idx_ref.shape[2:])


def _nsa_decode_select(q, cmp_kv, n_heads, n_groups, pos):
    B = q.shape[0]
    G = n_groups
    gs = n_heads // G
    dh = HEAD_DIM
    R = cmp_kv.shape[2]
    cur = pos // SEL_BLOCK
    assert pos % SEL_BLOCK == 0 and cur >= SEL_TOPK
    c_start = np.arange(R) * CMP_STRIDE
    s_start = np.arange(cur) * SEL_BLOCK
    nc = (pos + 1 - CMP_LEN) // CMP_STRIDE + 1
    overlap = ((c_start[:, None] <= s_start[None, :] + SEL_BLOCK - 1)
               & (c_start[:, None] + CMP_LEN - 1 >= s_start[None, :])
               & (np.arange(R)[:, None] < nc))
    overlap = jnp.asarray(overlap.astype(np.float32), BF16)
    return pl.pallas_call(
        functools.partial(_nsa_select_kernel, gs=gs, pos=pos),
        grid=(B, G),
        in_specs=[pl.BlockSpec(memory_space=pltpu.SMEM),
                  pl.BlockSpec((1, 1, gs * dh), lambda b, g: (b, 0, g)),
                  pl.BlockSpec((1, 1, R, dh), lambda b, g: (b, g, 0, 0)),
                  pl.BlockSpec((1, 1, R, dh), lambda b, g: (b, G + g, 0, 0)),
                  pl.BlockSpec((R, cur), lambda b, g: (0, 0))],
        out_specs=[pl.BlockSpec((1, 1, gs, dh), lambda b, g: (b, g, 0, 0)),
                   pl.BlockSpec((1, 1, 8, 128), lambda b, g: (b, g, 0, 0))],
        out_shape=[jax.ShapeDtypeStruct((B, G, gs, dh), F32), jax.ShapeDtypeStruct((B, G, 8, 128), jnp.int32)],
        compiler_params=_params("parallel", "parallel"),
        name="nsa_decode_select",
    )(_alibi_slopes(n_heads), q, cmp_kv, cmp_kv, overlap)


def _pick_group(ref, g, n_groups):
    out = ref[:, 0, :]
    for gg in range(1, n_groups):
        out = jnp.where(g == gg, ref[:, gg, :], out)
    return out


def _nsa_attend_kernel(pool_blk_ref, blk_ref, slopes_ref, q_ref, gates_ref, ocmp_ref, ksn_ref, vsn_ref,
                       kwn_ref, vwn_ref, kw_ref, vw_ref, *rest, n_groups, gs, pos):
    cache_refs, o_ref = rest[:-1], rest[-1]
    b = pl.program_id(0)
    g = pl.program_id(1)
    dh = HEAD_DIM
    n_sel = len(cache_refs) // 2
    qs = _split_heads(q_ref[0], gs)
    slope = _group_slopes(slopes_ref, g, gs)

    def attend(k_past, v_past, dist, ok, k_new, v_new):
        s = jnp.where(ok, _dot_nt(qs, k_past) * SCALE - slope * dist.astype(F32), NEG)
        s_new = _dot_nt(qs, jnp.broadcast_to(k_new, (gs, dh)))[:, 0:1] * SCALE
        m = jnp.maximum(jnp.max(s, axis=-1, keepdims=True), s_new)
        p = jnp.exp(s - m)
        p_new = jnp.exp(s_new - m)
        den = jnp.sum(p, axis=-1, keepdims=True) + p_new
        acc = _dot(p, v_past) + _mx(p_new).astype(F32) * _mx(v_new).astype(F32)
        return acc / den

    lane = lax.broadcasted_iota(jnp.int32, (1, SEL_BLOCK), 1)
    dists = [pos - (blk_ref[(b * n_groups + g) * n_sel + j] * SEL_BLOCK + lane) for j in range(n_sel)]
    dist = jnp.concatenate(dists, axis=1)
    k_sel = jnp.concatenate([_pick_group(cache_refs[2 * j], g, n_groups) for j in range(n_sel)], axis=0)
    v_sel = jnp.concatenate([_pick_group(cache_refs[2 * j + 1], g, n_groups) for j in range(n_sel)], axis=0)
    o_sel = attend(k_sel, v_sel, dist, dist >= 0, ksn_ref[0], vsn_ref[0])

    n_buf = kw_ref.shape[0]
    dist_w = n_buf - lax.broadcasted_iota(jnp.int32, (1, n_buf), 1)
    o_win = attend(_pick_group(kw_ref, g, n_groups), _pick_group(vw_ref, g, n_groups), dist_w, dist_w < WINDOW,
                   kwn_ref[0], vwn_ref[0])

    gate = jax.nn.sigmoid(gates_ref[0, 0])
    o = gate[:, 0:1] * ocmp_ref[0, 0] + gate[:, 1:2] * o_sel + gate[:, 2:3] * o_win
    for s in range(gs):
        o_ref[0, :, s * dh:(s + 1) * dh] = o[s:s + 1].astype(o_ref.dtype)


def _nsa_decode_attend(q, gates, o_cmp, nsa_new, win_new, cache, win_state, layer, page_table, top_idx,
                       n_heads, n_groups, pos):
    B, n_pages = page_table.shape
    page = cache.shape[2]
    assert cache.shape[3] == 4
    G = n_groups
    gs = n_heads // G
    dh = HEAD_DIM
    per_page = page // SEL_BLOCK
    n_buf = win_state.shape[2]
    assert page % SEL_BLOCK == 0 and pos >= n_buf
    blk = top_idx[:, :, 0, :SEL_TOPK]
    phys = jnp.take_along_axis(page_table[:, None, :], (blk // per_page).reshape(B, 1, -1), axis=2)
    pool_blk = phys.reshape(B, G, SEL_TOPK) * per_page + blk % per_page
    n_sel = SEL_TOPK

    def cache_spec(j, kind):
        def imap(b, g, pool_blk_ref, blk_ref):
            pb = pool_blk_ref[(b * G + g) * n_sel + j]
            return (layer, pb // per_page, pb % per_page, kind, 0, 0)
        return pl.BlockSpec((None, None, SEL_BLOCK, None, G, dh), imap)

    cache_specs = []
    for j in range(n_sel):
        cache_specs += [cache_spec(j, 2), cache_spec(j, 3)]
    tok = lambda col0: pl.BlockSpec((1, 1, dh), lambda b, g, *_: (b, 0, col0 + g))
    win = lambda kv: pl.BlockSpec((None, None, n_buf, None, G, dh), lambda b, g, *_: (layer, b, 0, kv, 0, 0))
    grid_spec = pltpu.PrefetchScalarGridSpec(
        num_scalar_prefetch=2,
        grid=(B, G),
        in_specs=[pl.BlockSpec(memory_space=pltpu.SMEM),
                  pl.BlockSpec((1, 1, gs * dh), lambda b, g, *_: (b, 0, g)),
                  pl.BlockSpec((1, 1, gs, 3), lambda b, g, *_: (b, g, 0, 0)),
                  pl.BlockSpec((1, 1, gs, dh), lambda b, g, *_: (b, g, 0, 0)),
                  tok(2 * G), tok(3 * G), tok(0), tok(G), win(0), win(1)] + cache_specs,
        out_specs=pl.BlockSpec((1, 1, gs * dh), lambda b, g, *_: (b, 0, g)))
    return pl.pallas_call(
        functools.partial(_nsa_attend_kernel, n_groups=G, gs=gs, pos=pos),
        grid_spec=grid_spec,
        out_shape=jax.ShapeDtypeStruct((B, 1, n_heads * dh), BF16),
        compiler_params=_params("parallel", "arbitrary"),
        name="nsa_decode_attend",
    )(pool_blk.reshape(-1), blk.reshape(-1), _alibi_slopes(n_heads), q, gates, o_cmp,
      nsa_new, nsa_new, win_new, win_new, win_state, win_state, *([cache] * (2 * n_sel)))


def _trunk(x3, start, past, ln_g, ln_b, ffn_gate, ffn_up, ffn_down, attn_w_in, attn_w_out,
           cmp_pe, cmp_w1, cmp_w2, pool_w, pool_scale):
    B, T, D = x3.shape
    M = B * T
    depth = ffn_gate.shape[0]
    d_ff = ffn_gate.shape[-1]
    alpha = (2.0 * depth) ** 0.25
    decode = past is not None
    dh = HEAD_DIM
    n_heads = D // dh
    H_A = n_heads // 2
    NH = n_heads - H_A
    G = cmp_pe.shape[1]
    gs = NH // G
    moba_w, nsa_w, kvw = H_A * dh, NH * dh, G * dh
    tm = min(M, 1024)
    tm_ln = min(M, 256)
    pool_w_mx = pool_w.astype(MXU_DTYPE)
    if decode:
        assert T == 1
        cache_moba_kv, cache_nsa_kv, state_nsa_win, state_pool, page_table = past

    x = x3.reshape(M, D)
    xb = x.astype(BF16)
    new_moba, new_nsa, new_win, new_pool = [], [], [], []

    def ffn(x, xb, layer, sub, ln_idx):
        hidden = _ffn_up(xb, ffn_gate, ffn_up, layer, sub, tm)
        y = _matmul(hidden, ffn_down, (layer, sub), 0, D, F32, tm=tm, tn=256, kchunk=d_ff // 2, name="ffn_down")
        return _res_ln(y, x, ln_g[layer, ln_idx], ln_b[layer, ln_idx], alpha=alpha, coef=0.5, tm=tm_ln)

    for layer in range(depth):
        x, xb = ffn(x, xb, layer, 0, 0)
        if layer % 2 == 0:
            a = layer // 2
            proj = functools.partial(_matmul, xb, attn_w_in, (a,), tm=tm, name="in_proj")
            q_a = proj(0, moba_w, BF16, tn=256)
            moba_kv = proj(moba_w, 2 * moba_w, F32, tn=256)
            q_b = proj(3 * moba_w, nsa_w, BF16, tn=256)
            c0 = 3 * moba_w + nsa_w
            nsa_kv = proj(c0, 4 * kvw, F32, tn=256)
            win_kv = proj(c0 + 4 * kvw, 2 * kvw, F32, tn=256)
            gates = proj(c0 + 6 * kvw, 3 * NH, F32, tn=128)
            moba_new = moba_kv.reshape(B, T, 2, H_A, dh)
            nsa_new = nsa_kv.reshape(B, T, 4, G, dh)
            win_new = win_kv.reshape(B, T, 2, G, dh)
            q_a3, q_b3 = q_a.reshape(B, T, moba_w), q_b.reshape(B, T, nsa_w)
            moba_kv3, nsa_kv3, win_kv3 = (moba_kv.reshape(B, T, 2 * moba_w), nsa_kv.reshape(B, T, 4 * kvw),
                                          win_kv.reshape(B, T, 2 * kvw))
            if not decode:
                o_a = _moba_prompt(q_a3, moba_kv3, H_A)
                kv16 = nsa_new[:, :, 0:2].transpose(0, 2, 3, 1, 4).reshape(B, 2, G, T // 16, 16 * dh)
                cmp_kv = _compress_prompt(kv16, cmp_pe[a].reshape(2, 2, 16 * dh), cmp_w1, cmp_w2, a)
                gates_g = gates.reshape(B, T, 3, G, gs).transpose(0, 3, 1, 2, 4).reshape(B, G, T, 3 * gs)
                o_b = _nsa_prompt(q_b3, gates_g, cmp_kv, nsa_kv3, win_kv3, NH, G)
                win_state = win_new[:, -min(WINDOW, T):]
            else:
                top_a = _moba_decode_select(q_a3, cache_moba_kv, a, page_table, H_A)
                o_a = _moba_decode_attend(q_a3, moba_kv3, cache_moba_kv, a, page_table, top_a, H_A, start)
                cmp_kv = _nsa_decode_compress(cache_nsa_kv, a, page_table, cmp_pe, cmp_w1, cmp_w2, G)
                o_cmp, top_b = _nsa_decode_select(q_b3, cmp_kv, NH, G, start)
                gates_g = gates.reshape(B, 3, G, gs).transpose(0, 2, 3, 1)
                o_b = _nsa_decode_attend(q_b3, gates_g, o_cmp, nsa_kv3, win_kv3, cache_nsa_kv, state_nsa_win, a,
                                         page_table, top_b, NH, G, start)
                win_state = jnp.concatenate([state_nsa_win[a][:, T:], win_new], axis=1)
            mixed = jnp.concatenate([o_a, o_b], axis=-1).reshape(M, D)
            new_moba.append(moba_new)
            new_nsa.append(nsa_new)
            new_win.append(win_state)
            y = _matmul(mixed, attn_w_out, (a,), 0, D, F32, tm=tm, tn=256, name="out_proj")
            x, xb = _res_ln(y, x, ln_g[layer, 1], ln_b[layer, 1], alpha=alpha, coef=1.0, tm=tm_ln)
        else:
            p = layer // 2
            x3c = x.reshape(B, T, D)
            if decode:
                prev = jnp.concatenate([state_pool[p], x3c], axis=1)
                new_pool.append(prev[:, -POOL_BUF:])
                tt = T
            else:
                prev = x3c
                new_pool.append(x3c[:, -POOL_BUF:])
                tt = 256
            xo, xbo = _pool_mixer_ln(x3c, prev, pool_w_mx[p], pool_scale[p], ln_g[layer, 1], ln_b[layer, 1],
                                     alpha=alpha, start=start, tt=tt, decode=decode)
            x, xb = xo.reshape(M, D), xbo.reshape(M, D)
        x, xb = ffn(x, xb, layer, 1, 2)
    return (x.reshape(B, T, D), jnp.stack(new_moba), jnp.stack(new_nsa), jnp.stack(new_win),
            jnp.stack(new_pool))


def kernel(x_prompt, x_sample, cache_moba_kv, cache_nsa_kv, state_nsa_win, state_pool, page_table,
           ln_g, ln_b, ffn_gate, ffn_up, ffn_down, attn_w_in, attn_w_out, cmp_pe, cmp_w1, cmp_w2,
           pool_w, pool_scale):
    weights = (ln_g, ln_b, ffn_gate, ffn_up, ffn_down, attn_w_in, attn_w_out, cmp_pe, cmp_w1, cmp_w2,
               pool_w, pool_scale)
    past_len = page_table.shape[1] * cache_moba_kv.shape[2]
    y_p, moba_p, nsa_p, win_p, pool_p = _trunk(x_prompt, 0, None, *weights)
    y_s, moba_s, nsa_s, win_s, pool_s = _trunk(
        x_sample, past_len, (cache_moba_kv, cache_nsa_kv, state_nsa_win, state_pool, page_table), *weights)
    return (y_p, y_s, moba_p, nsa_p, win_p, pool_p, moba_s, nsa_s, win_s, pool_s)
```

```python
import functools

import jax
import jax.numpy as jnp
import numpy as np
from jax import lax
from jax.experimental import pallas as pl
from jax.experimental.pallas import tpu as pltpu

F32 = jnp.float32
BF16 = jnp.bfloat16
MXU_DTYPE = BF16

HEAD_DIM = 128
MOBA_BLOCK = 256
MOBA_TOPK = 3
CMP_LEN = 32
CMP_STRIDE = 16
SEL_BLOCK = 64
SEL_TOPK = 15
WINDOW = 512
POOL_WINDOWS = (2, 4, 8, 16)
POOL_BUF = max(POOL_WINDOWS) - 1
LN_EPS = 1e-5
SCALE = HEAD_DIM ** -0.5
NEG = -1e30
LOWEST = -3e38
NSA_Q_TILE = 256
NSA_KEY_CHUNK = 1024
MOBA_KEY_CHUNK = 1024
MOBA_HEADS_PER_STEP = 2
SUBLANES = 8
DECODE_PAGES_PER_STEP = 8

V7X_VMEM_BYTES = 64 * 1024 * 1024
VMEM_LIMIT = V7X_VMEM_BYTES - 8 * 1024 * 1024


def _params(*sem):
    return pltpu.CompilerParams(dimension_semantics=sem, vmem_limit_bytes=VMEM_LIMIT)


def _mx(a):
    return a.astype(MXU_DTYPE)


def _dot(a, b):
    return jnp.dot(_mx(a), _mx(b), preferred_element_type=F32)


def _dot_tn(a, b):
    return lax.dot_general(_mx(a), _mx(b), (((0,), (0,)), ((), ())), preferred_element_type=F32)


def _dot_nt(a, b):
    return lax.dot_general(_mx(a), _mx(b), (((1,), (1,)), ((), ())), preferred_element_type=F32)


def _alibi_slopes(n):
    return jnp.asarray(np.exp2(-8.0 * np.arange(1, n + 1) / n), dtype=F32)


def _layernorm_rows(z, g, b):
    mu = jnp.mean(z, axis=-1, keepdims=True)
    zc = z - mu
    var = jnp.mean(zc * zc, axis=-1, keepdims=True)
    return zc * lax.rsqrt(var + LN_EPS) * g + b


def _ffn_up_kernel(x_ref, wg_ref, wu_ref, o_ref):
    x = x_ref[...]
    g = _dot(x, wg_ref[...])
    u = _dot(x, wu_ref[...])
    o_ref[...] = (g * jax.nn.sigmoid(g) * u).astype(o_ref.dtype)


def _ffn_up(xb, w_gate, w_up, layer, sub, tm, tn=256):
    M, D = xb.shape
    F = w_gate.shape[-1]
    assert M % tm == 0 and F % tn == 0
    wspec = pl.BlockSpec((None, None, D, tn), lambda i, j: (layer, sub, 0, j))
    return pl.pallas_call(
        _ffn_up_kernel,
        grid=(M // tm, F // tn),
        in_specs=[pl.BlockSpec((tm, D), lambda i, j: (i, 0)), wspec, wspec],
        out_specs=pl.BlockSpec((tm, tn), lambda i, j: (i, j)),
        out_shape=jax.ShapeDtypeStruct((M, F), BF16),
        compiler_params=_params("parallel", "arbitrary"),
        name="ffn_up",
    )(xb, w_gate, w_up)


def _matmul_kernel(x_ref, w_ref, o_ref, *, kchunk):
    K = x_ref.shape[1]
    acc = None
    for k0 in range(0, K, kchunk):
        part = _dot(x_ref[:, k0:k0 + kchunk], w_ref[k0:k0 + kchunk, :])
        acc = part if acc is None else acc + part
    o_ref[...] = acc[:, :o_ref.shape[1]].astype(o_ref.dtype)


def _matmul(xb, w, widx, col0, ncols, out_dtype, *, tm, tn, kchunk=None, name):
    M, K = xb.shape
    assert M % tm == 0 and col0 % tn == 0
    kchunk = K if kchunk is None else kchunk
    assert K % kchunk == 0
    if ncols % tn == 0:
        nj, out_tn = ncols // tn, tn
    else:
        assert ncols < tn
        nj, out_tn = 1, ncols
    j0 = col0 // tn
    lead = (None,) * len(widx)
    x_bytes = tm * K * xb.dtype.itemsize
    x_mode = {"pipeline_mode": pl.Buffered(1)} if 4 * x_bytes > VMEM_LIMIT else {}
    x_spec = pl.BlockSpec((tm, K), lambda i, j: (i, 0), **x_mode)
    return pl.pallas_call(
        functools.partial(_matmul_kernel, kchunk=kchunk),
        grid=(M // tm, nj),
        in_specs=[x_spec, pl.BlockSpec(lead + (K, tn), lambda i, j: tuple(widx) + (0, j0 + j))],
        out_specs=pl.BlockSpec((tm, out_tn), lambda i, j: (i, j)),
        out_shape=jax.ShapeDtypeStruct((M, ncols), out_dtype),
        compiler_params=_params("parallel", "arbitrary"),
        name=name,
    )(xb, w)


def _res_ln_kernel(y_ref, res_ref, g_ref, b_ref, o_ref, ob_ref, *, alpha, coef):
    z = alpha * res_ref[...] + coef * y_ref[...]
    y = _layernorm_rows(z, g_ref[...], b_ref[...])
    o_ref[...] = y
    ob_ref[...] = y.astype(ob_ref.dtype)


def _res_ln(y, res, g, b, *, alpha, coef, tm):
    M, D = y.shape
    assert M % tm == 0
    row = pl.BlockSpec((1, D), lambda i: (0, 0))
    tile = pl.BlockSpec((tm, D), lambda i: (i, 0))
    return pl.pallas_call(
        functools.partial(_res_ln_kernel, alpha=alpha, coef=coef),
        grid=(M // tm,),
        in_specs=[tile, tile, row, row],
        out_specs=[tile, tile],
        out_shape=[jax.ShapeDtypeStruct((M, D), F32), jax.ShapeDtypeStruct((M, D), BF16)],
        compiler_params=_params("parallel"),
        name="res_ln",
    )(y, res, g.reshape(1, D), b.reshape(1, D))


def _pool_kernel(x_ref, prev_ref, w_ref, sc_ref, g_ref, b_ref, o_ref, ob_ref, *,
                 alpha, start, tt, decode, group):
    ti = pl.program_id(1)
    zs = []
    for gi, win in enumerate(POOL_WINDOWS):
        c0, c1 = gi * group, (gi + 1) * group
        prev = prev_ref[0, :, c0:c1]
        if decode:
            xg = prev[POOL_BUF:POOL_BUF + 1]
            wsum = jnp.sum(prev[POOL_BUF + 1 - win:], axis=0, keepdims=True)
            cnt = jnp.full((1, 1), float(min(win, start + 1)), F32)
        else:
            xg = x_ref[0, :, c0:c1]
            prev = jnp.where(ti == 0, 0.0, prev)
            s = jnp.concatenate([prev, xg], axis=0)
            span = 1
            while span < win:
                s = s + pltpu.roll(s, span, axis=0)
                span *= 2
            wsum = s[POOL_BUF + 1:]
            pos = start + ti * tt + lax.broadcasted_iota(jnp.int32, (tt, 1), 0)
            cnt = jnp.minimum(win, pos + 1).astype(F32)
        pooled = wsum / cnt - xg
        y = _dot(pooled, w_ref[gi]) * sc_ref[:, c0:c1]
        zs.append(alpha * xg + y)
    d = group * len(POOL_WINDOWS)
    mu = sum(jnp.sum(z, axis=-1, keepdims=True) for z in zs) / d
    var = sum(jnp.sum((z - mu) * (z - mu), axis=-1, keepdims=True) for z in zs) / d
    inv = lax.rsqrt(var + LN_EPS)
    for gi, z in enumerate(zs):
        c0, c1 = gi * group, (gi + 1) * group
        y = (z - mu) * inv * g_ref[:, c0:c1] + b_ref[:, c0:c1]
        o_ref[0, :, c0:c1] = y
        ob_ref[0, :, c0:c1] = y.astype(ob_ref.dtype)


def _pool_mixer_ln(x3, prev3, w_pool, scale, g, b, *, alpha, start, tt, decode):
    B, T, D = x3.shape
    group = D // len(POOL_WINDOWS)
    assert T % tt == 0 and (decode or tt % 16 == 0)
    per = tt // 16 if not decode else 0
    if decode:
        prev_map = lambda bi, ti: (bi, 0, 0)
    else:
        prev_map = lambda bi, ti: (bi, jnp.maximum(ti * per - 1, 0), 0)
    row = pl.BlockSpec((1, D), lambda bi, ti: (0, 0))
    tile = pl.BlockSpec((1, tt, D), lambda bi, ti: (bi, ti, 0))
    return pl.pallas_call(
        functools.partial(_pool_kernel, alpha=alpha, start=start, tt=tt, decode=decode, group=group),
        grid=(B, T // tt),
        in_specs=[tile, pl.BlockSpec((1, 16, D), prev_map),
                  pl.BlockSpec(w_pool.shape, lambda bi, ti: (0, 0, 0)), row, row, row],
        out_specs=[tile, tile],
        out_shape=[jax.ShapeDtypeStruct((B, T, D), F32), jax.ShapeDtypeStruct((B, T, D), BF16)],
        compiler_params=_params("parallel", "arbitrary"),
        name="pool_mixer_ln",
    )(x3, prev3, w_pool, scale.reshape(1, D), g.reshape(1, D), b.reshape(1, D))


def _topk_mask_t(score_t, row, n_valid, k):
    n = score_t.shape[0]
    masked = jnp.where(row < n_valid, score_t, NEG)
    rank = jnp.zeros(score_t.shape, jnp.int32)
    for j in range(n):
        other = masked[j:j + 1, :]
        ahead = (other > masked) | ((other == masked) & (j < row))
        rank = rank + ahead.astype(jnp.int32)
    return (rank < k) & (row < n_valid)


def _online_update(m_ref, l_ref, acc_ref, s, v):
    m_old = m_ref[...]
    m_new = jnp.maximum(m_old, jnp.max(s, axis=-1, keepdims=True))
    p = jnp.exp(s - m_new)
    a = jnp.exp(m_old - m_new)
    l_ref[...] = a * l_ref[...] + jnp.sum(p, axis=-1, keepdims=True)
    acc_ref[...] = a * acc_ref[...] + _dot(p, v)
    m_ref[...] = m_new


def _moba_kernel(slopes_ref, q_ref, k_ref, v_ref, ex_ref, o_ref, m_ref, l_ref, acc_ref, kmean_ref, *, nb, hpg):
    hg = pl.program_id(1)
    qi = pl.program_id(2)
    blk = MOBA_BLOCK
    dh = HEAD_DIM
    T = nb * blk
    kc = min(T, MOBA_KEY_CHUNK)
    heads = [(e * dh, (e + 1) * dh) for e in range(hpg)]
    slopes = [slopes_ref[hg * hpg + e] for e in range(hpg)]
    qs = [q_ref[0, :, c0:c1] for c0, c1 in heads]
    row = lax.broadcasted_iota(jnp.int32, (nb, blk), 0)

    @pl.when(qi == 0)
    def _():
        for e, (c0, c1) in enumerate(heads):
            kmean_ref[e] = jnp.concatenate(
                [jnp.sum(k_ref[0, n * blk:(n + 1) * blk, c0:c1], axis=0, keepdims=True) for n in range(nb)],
                axis=0) * (1.0 / blk)

    sel_keys = []
    for e, (c0, c1) in enumerate(heads):
        gate_t = _dot_nt(kmean_ref[e], qs[e])
        sel_t = _topk_mask_t(gate_t, row, qi, MOBA_TOPK) | (row == qi)
        sel_keys.append(_dot_tn(sel_t.astype(F32), ex_ref[...]))
    qpos = qi * blk + lax.broadcasted_iota(jnp.int32, (blk, 1), 0)

    m_ref[...] = jnp.full(m_ref.shape, NEG, F32)
    l_ref[...] = jnp.zeros(l_ref.shape, F32)
    acc_ref[...] = jnp.zeros(acc_ref.shape, F32)
    for c in reversed(range(T // kc)):
        @pl.when(c * kc <= qi * blk)
        def _():
            dist = qpos - (c * kc + lax.broadcasted_iota(jnp.int32, (1, kc), 1))
            causal = dist >= 0
            distf = dist.astype(F32)
            for e, (c0, c1) in enumerate(heads):
                ok = (sel_keys[e][:, c * kc:(c + 1) * kc] > 0.5) & causal
                s = (_dot_nt(qs[e], k_ref[0, c * kc:(c + 1) * kc, c0:c1]) * SCALE - slopes[e] * distf
                     + jnp.where(ok, 0.0, NEG))
                _online_update(m_ref.at[e], l_ref.at[e], acc_ref.at[e], s, v_ref[0, c * kc:(c + 1) * kc, c0:c1])
    for e, (c0, c1) in enumerate(heads):
        o_ref[0, :, c0:c1] = (acc_ref[e] / l_ref[e]).astype(o_ref.dtype)


def _moba_prompt(q, kv, n_heads):
    B, T, _ = q.shape
    hpg = MOBA_HEADS_PER_STEP
    assert T % MOBA_BLOCK == 0 and T % min(T, MOBA_KEY_CHUNK) == 0 and MOBA_KEY_CHUNK % MOBA_BLOCK == 0
    assert n_heads % hpg == 0
    nb = T // MOBA_BLOCK
    dh = HEAD_DIM
    w = hpg * dh
    expand = (np.arange(T)[None, :] // MOBA_BLOCK) == np.arange(nb)[:, None]
    expand = jnp.asarray(expand.astype(np.float32), BF16)
    return pl.pallas_call(
        functools.partial(_moba_kernel, nb=nb, hpg=hpg),
        grid=(B, n_heads // hpg, nb),
        in_specs=[pl.BlockSpec(memory_space=pltpu.SMEM),
                  pl.BlockSpec((1, MOBA_BLOCK, w), lambda b, h, i: (b, i, h)),
                  pl.BlockSpec((1, T, w), lambda b, h, i: (b, 0, h)),
                  pl.BlockSpec((1, T, w), lambda b, h, i: (b, 0, n_heads // hpg + h)),
                  pl.BlockSpec((nb, T), lambda b, h, i: (0, 0))],
        out_specs=pl.BlockSpec((1, MOBA_BLOCK, w), lambda b, h, i: (b, i, h)),
        out_shape=jax.ShapeDtypeStruct((B, T, n_heads * dh), BF16),
        scratch_shapes=[pltpu.VMEM((hpg, MOBA_BLOCK, 1), F32), pltpu.VMEM((hpg, MOBA_BLOCK, 1), F32),
                        pltpu.VMEM((hpg, MOBA_BLOCK, dh), F32), pltpu.VMEM((hpg, nb, dh), F32)],
        compiler_params=_params("parallel", "parallel", "arbitrary"),
        name="moba_prompt",
    )(_alibi_slopes(n_heads), q, kv, kv, expand)


def _compress_kernel(x_ref, pe_ref, w1_ref, w2_ref, o_ref, *, nc):
    x = x_ref[0, 0, 0]
    half = x.shape[1]
    lo = _dot(x + pe_ref[0, 0:1, :], w1_ref[:half, :])
    hi = _dot(x + pe_ref[0, 1:2, :], w1_ref[half:, :])
    rows = x.shape[0]
    hid = lo + pltpu.roll(hi, rows - 1, axis=0)
    y = _dot(hid * jax.nn.sigmoid(hid), w2_ref[...])
    ridx = lax.broadcasted_iota(jnp.int32, y.shape, 0)
    o_ref[0, 0, 0] = jnp.where(ridx < nc, y, 0.0)


def _compress_prompt(kv16, pe, w1, w2, layer):
    B, two, G, R, W = kv16.shape
    nc = R - 1
    hidden = w1.shape[-1]
    return pl.pallas_call(
        functools.partial(_compress_kernel, nc=nc),
        grid=(B, two, G),
        in_specs=[pl.BlockSpec((1, 1, 1, R, W), lambda b, c, g: (b, c, g, 0, 0)),
                  pl.BlockSpec((1, 2, W), lambda b, c, g: (c, 0, 0)),
                  pl.BlockSpec((None, None, 2 * W, hidden), lambda b, c, g: (layer, c, 0, 0)),
                  pl.BlockSpec((None, None, hidden, HEAD_DIM), lambda b, c, g: (layer, c, 0, 0))],
        out_specs=pl.BlockSpec((1, 1, 1, R, HEAD_DIM), lambda b, c, g: (b, c, g, 0, 0)),
        out_shape=jax.ShapeDtypeStruct((B, two, G, R, HEAD_DIM), F32),
        compiler_params=_params("parallel", "parallel", "parallel"),
        name="nsa_compress",
    )(kv16, pe, w1, w2)


def _nsa_kernel(slopes_ref, q_ref, gates_ref, kc_ref, vc_ref, ks_ref, vs_ref, kw_ref, vw_ref,
                ov_ref, ex_ref, o_ref, m_ref, l_ref, acc_ref, *, T, gs):
    g = pl.program_id(1)
    qi = pl.program_id(2)
    tq = NSA_Q_TILE
    dh = HEAD_DIM
    q = q_ref[0]
    qs = jnp.concatenate([q[:, s * dh:(s + 1) * dh] for s in range(gs)], axis=0)
    slopes = [slopes_ref[g * gs + s] for s in range(gs)]
    heads = [(s * tq, (s + 1) * tq) for s in range(gs)]
    tpos = qi * tq + lax.broadcasted_iota(jnp.int32, (tq, 1), 0)

    ncp = kc_ref.shape[3]
    c_end = lax.broadcasted_iota(jnp.int32, (1, ncp), 1) * CMP_STRIDE + (CMP_LEN - 1)
    dist_c = tpos - c_end
    cmask = dist_c >= 0
    bias_c = jnp.where(cmask, 0.0, NEG)
    distf_c = dist_c.astype(F32)
    s_all = _dot_nt(qs, kc_ref[0, 0, 0])
    p_list = []
    for s, (r0, r1) in enumerate(heads):
        sc = s_all[r0:r1] * SCALE - slopes[s] * distf_c + bias_c
        e = jnp.where(cmask, jnp.exp(sc - jnp.max(sc, axis=-1, keepdims=True)), 0.0)
        den = jnp.sum(e, axis=-1, keepdims=True)
        p_list.append(e / jnp.where(den > 0.0, den, 1.0))
    p_c = _mx(jnp.concatenate(p_list, axis=0))
    o_cmp = _dot(p_c, vc_ref[0, 0, 0])

    imp_cols = _dot_nt(ov_ref[...], p_c)
    imp_t = imp_cols[:, 0:tq]
    for r0, r1 in heads[1:]:
        imp_t = imp_t + imp_cols[:, r0:r1]
    ns = imp_t.shape[0]
    rowj = lax.broadcasted_iota(jnp.int32, (ns, tq), 0)
    cur_t = (qi * tq + lax.broadcasted_iota(jnp.int32, (1, tq), 1)) // SEL_BLOCK
    sel_t = _topk_mask_t(imp_t, rowj, cur_t, SEL_TOPK).astype(F32)
    sel_keys = _dot_tn(sel_t, ex_ref[...])

    m_ref[...] = jnp.full(m_ref.shape, NEG, F32)
    l_ref[...] = jnp.zeros(l_ref.shape, F32)
    acc_ref[...] = jnp.zeros(acc_ref.shape, F32)
    kc = NSA_KEY_CHUNK
    for c in reversed(range(T // kc)):
        @pl.when(c * kc < (qi + 1) * tq)
        def _():
            kpos = c * kc + lax.broadcasted_iota(jnp.int32, (1, kc), 1)
            dist = tpos - kpos
            own = (kpos // SEL_BLOCK) == (tpos // SEL_BLOCK)
            ok = ((sel_keys[:, c * kc:(c + 1) * kc] > 0.5) | own) & (dist >= 0)
            bias = jnp.where(ok, 0.0, NEG)
            distf = dist.astype(F32)
            s_all = _dot_nt(qs, ks_ref[0, c * kc:(c + 1) * kc, :])
            m_all = m_ref[...]
            ps, maxes, sums = [], [], []
            for s, (r0, r1) in enumerate(heads):
                sc = s_all[r0:r1] * SCALE - slopes[s] * distf + bias
                m_new = jnp.maximum(m_all[r0:r1], jnp.max(sc, axis=-1, keepdims=True))
                p = jnp.exp(sc - m_new)
                maxes.append(m_new)
                sums.append(jnp.sum(p, axis=-1, keepdims=True))
                ps.append(_mx(p))
            m_new = jnp.concatenate(maxes, axis=0)
            a = jnp.exp(m_all - m_new)
            pv = _dot(jnp.concatenate(ps, axis=0), vs_ref[0, c * kc:(c + 1) * kc, :])
            l_ref[...] = a * l_ref[...] + jnp.concatenate(sums, axis=0)
            acc_ref[...] = a * acc_ref[...] + pv
            m_ref[...] = m_new
    o_sel = acc_ref[...] / l_ref[...]

    span = WINDOW + tq
    w0 = pl.multiple_of(jnp.maximum(qi * tq - WINDOW, 0), tq)
    kpos = w0 + lax.broadcasted_iota(jnp.int32, (1, span), 1)
    dist = tpos - kpos
    bias = jnp.where((dist >= 0) & (dist < WINDOW), 0.0, NEG)
    distf = dist.astype(F32)
    s_all = _dot_nt(qs, kw_ref[0, pl.ds(w0, span), :])
    ps = []
    for s, (r0, r1) in enumerate(heads):
        sc = s_all[r0:r1] * SCALE - slopes[s] * distf + bias
        e = jnp.exp(sc - jnp.max(sc, axis=-1, keepdims=True))
        ps.append(_mx(e / jnp.sum(e, axis=-1, keepdims=True)))
    o_win = _dot(jnp.concatenate(ps, axis=0), vw_ref[0, pl.ds(w0, span), :])

    gate = jax.nn.sigmoid(gates_ref[0, 0])
    for s, (r0, r1) in enumerate(heads):
        o = (gate[:, s:s + 1] * o_cmp[r0:r1] + gate[:, gs + s:gs + s + 1] * o_sel[r0:r1]
             + gate[:, 2 * gs + s:2 * gs + s + 1] * o_win[r0:r1])
        o_ref[0, :, s * dh:(s + 1) * dh] = o.astype(o_ref.dtype)


def _nsa_tables(T, rows_cmp):
    ns = -(-T // SEL_BLOCK)
    c_start = np.arange(rows_cmp) * CMP_STRIDE
    s_start = np.arange(ns) * SEL_BLOCK
    nc = (T - CMP_LEN) // CMP_STRIDE + 1
    overlap = ((c_start[:, None] <= s_start[None, :] + SEL_BLOCK - 1)
               & (c_start[:, None] + CMP_LEN - 1 >= s_start[None, :])
               & (np.arange(rows_cmp)[:, None] < nc))
    expand = (np.arange(T)[None, :] // SEL_BLOCK) == np.arange(ns)[:, None]
    return jnp.asarray(overlap.T.astype(np.float32), BF16), jnp.asarray(expand.astype(np.float32), BF16)


def _nsa_prompt(q, gates, cmp_kv, nsa_kv, win_kv, n_heads, n_groups):
    B, T, _ = q.shape
    G = n_groups
    gs = n_heads // G
    dh = HEAD_DIM
    tq = NSA_Q_TILE
    assert T % NSA_KEY_CHUNK == 0 and T >= WINDOW + tq and T % tq == 0
    R = cmp_kv.shape[3]
    overlap_t, expand = _nsa_tables(T, R)
    ns = overlap_t.shape[0]
    rows = gs * tq
    seq = lambda col: pl.BlockSpec((1, T, dh), lambda b, g, i: (b, 0, col(g)))
    return pl.pallas_call(
        functools.partial(_nsa_kernel, T=T, gs=gs),
        grid=(B, G, T // tq),
        in_specs=[pl.BlockSpec(memory_space=pltpu.SMEM),
                  pl.BlockSpec((1, tq, gs * dh), lambda b, g, i: (b, i, g)),
                  pl.BlockSpec((1, 1, tq, 3 * gs), lambda b, g, i: (b, g, i, 0)),
                  pl.BlockSpec((1, 1, 1, R, dh), lambda b, g, i: (b, 0, g, 0, 0)),
                  pl.BlockSpec((1, 1, 1, R, dh), lambda b, g, i: (b, 1, g, 0, 0)),
                  seq(lambda g: 2 * G + g), seq(lambda g: 3 * G + g),
                  seq(lambda g: g), seq(lambda g: G + g),
                  pl.BlockSpec((ns, R), lambda b, g, i: (0, 0)),
                  pl.BlockSpec((ns, T), lambda b, g, i: (0, 0))],
        out_specs=pl.BlockSpec((1, tq, gs * dh), lambda b, g, i: (b, i, g)),
        out_shape=jax.ShapeDtypeStruct((B, T, n_heads * dh), BF16),
        scratch_shapes=[pltpu.VMEM((rows, 1), F32), pltpu.VMEM((rows, 1), F32), pltpu.VMEM((rows, dh), F32)],
        compiler_params=_params("parallel", "parallel", "arbitrary"),
        name="nsa_prompt",
    )(_alibi_slopes(n_heads), q, gates, cmp_kv, cmp_kv, nsa_kv, nsa_kv, win_kv, win_kv, overlap_t, expand)


def _first_argmax(score, lane):
    mx = jnp.max(score, axis=-1, keepdims=True)
    return jnp.min(jnp.where(score == mx, lane, float(score.shape[1])), axis=-1, keepdims=True)


def _topk_indices(score, k, width):
    lane = lax.broadcasted_iota(jnp.int32, score.shape, 1).astype(F32)
    out_lane = lax.broadcasted_iota(jnp.int32, (score.shape[0], width), 1)
    out = jnp.zeros((score.shape[0], width), jnp.int32)
    for j in range(k):
        idx = _first_argmax(score, lane)
        out = jnp.where(out_lane == j, idx.astype(jnp.int32), out)
        score = jnp.where(lane == idx, LOWEST, score)
    return out


def _moba_select_kernel(pt_ref, q_ref, *rest, n_heads, n_steps, pages_per_block):
    cache_refs, (idx_ref, ksum_ref) = rest[:-2], rest[-2:]
    st = pl.program_id(1)
    blocks_per_step = len(cache_refs) // pages_per_block
    for i in range(blocks_per_step):
        total = None
        for r in range(pages_per_block):
            part = jnp.sum(cache_refs[i * pages_per_block + r][...], axis=0, keepdims=True)
            total = part if total is None else total + part
        ksum_ref[pl.ds(st * blocks_per_step + i, 1)] = total

    @pl.when(st == n_steps - 1)
    def _():
        dh = HEAD_DIM
        q = q_ref[0]
        gate = jnp.concatenate(
            [_dot_nt(q[:, h * dh:(h + 1) * dh], ksum_ref[:, h, :] * (1.0 / MOBA_BLOCK)) for h in range(n_heads)],
            axis=0)
        idx_ref[0] = _topk_indices(gate, MOBA_TOPK, idx_ref.shape[2])


def _moba_decode_select(q, cache, layer, page_table, n_heads):
    B, n_pages = page_table.shape
    page = cache.shape[2]
    width = n_heads * HEAD_DIM
    ppb = MOBA_BLOCK // page
    pps = DECODE_PAGES_PER_STEP
    assert MOBA_BLOCK % page == 0 and pps % ppb == 0 and n_pages % pps == 0 and n_pages // ppb >= MOBA_TOPK
    n_steps = n_pages // pps

    def cache_spec(i):
        return pl.BlockSpec((None, None, page, None, n_heads, HEAD_DIM),
                            lambda b, st, pt: (layer, pt[b * n_pages + st * pps + i], 0, 0, 0, 0))

    grid_spec = pltpu.PrefetchScalarGridSpec(
        num_scalar_prefetch=1,
        grid=(B, n_steps),
        in_specs=[pl.BlockSpec((1, 1, width), lambda b, st, pt: (b, 0, 0))] + [cache_spec(i) for i in range(pps)],
        out_specs=pl.BlockSpec((1, n_heads, 128), lambda b, st, pt: (b, 0, 0)),
        scratch_shapes=[pltpu.VMEM((n_pages // ppb, n_heads, HEAD_DIM), F32)])
    return pl.pallas_call(
        functools.partial(_moba_select_kernel, n_heads=n_heads, n_steps=n_steps, pages_per_block=ppb),
        grid_spec=grid_spec,
        out_shape=jax.ShapeDtypeStruct((B, n_heads, 128), jnp.int32),
        compiler_params=_params("parallel", "arbitrary"),
        name="moba_decode_select",
    )(page_table.reshape(-1), q, *([cache] * pps))


def _moba_attend_kernel(phys_ref, blk_ref, slopes_ref, q_ref, kn_ref, vn_ref, *rest, n_heads, pos, page, ppb):
    cache_refs, o_ref = rest[:-1], rest[-1]
    b = pl.program_id(0)
    h = pl.program_id(1)
    slope = slopes_ref[h]
    q = q_ref[0]
    n_sel = len(cache_refs) // (2 * ppb)
    rows = page * SUBLANES
    ridx = lax.broadcasted_iota(jnp.int32, (1, rows), 1)
    mine = (ridx % SUBLANES) == (h % SUBLANES)
    scores, values = [], []
    for j in range(n_sel):
        blk = blk_ref[(b * n_heads + h) * n_sel + j]
        for r in range(ppb):
            k_ref, v_ref = cache_refs[2 * (j * ppb + r)], cache_refs[2 * (j * ppb + r) + 1]
            dist = pos - (blk * MOBA_BLOCK + r * page + ridx // SUBLANES)
            s = _dot_nt(q, k_ref[...].reshape(rows, HEAD_DIM)) * SCALE - slope * dist.astype(F32)
            scores.append(jnp.where(mine, s, NEG))
            values.append(v_ref)
    s_new = jnp.sum(_mx(q).astype(F32) * _mx(kn_ref[0]).astype(F32), axis=-1, keepdims=True) * SCALE
    m = s_new
    for s in scores:
        m = jnp.maximum(m, jnp.max(s, axis=-1, keepdims=True))
    p_new = jnp.exp(s_new - m)
    den = p_new
    acc = _mx(p_new).astype(F32) * _mx(vn_ref[0]).astype(F32)
    for s, v_ref in zip(scores, values):
        p = jnp.exp(s - m)
        den = den + jnp.sum(p, axis=-1, keepdims=True)
        acc = acc + _dot(p, v_ref[...].reshape(rows, HEAD_DIM))
    o_ref[0] = (acc / den).astype(o_ref.dtype)


def _moba_decode_attend(q, kv_new, cache, layer, page_table, top_idx, n_heads, pos):
    B, n_pages = page_table.shape
    page = cache.shape[2]
    dh = HEAD_DIM
    ppb = MOBA_BLOCK // page
    blk = top_idx[:, :, :MOBA_TOPK]
    pages = blk[..., None] * ppb + jnp.arange(ppb, dtype=jnp.int32)
    phys = jnp.take_along_axis(page_table[:, None, :], pages.reshape(B, 1, -1), axis=2)
    n_sel = MOBA_TOPK

    assert n_heads % SUBLANES == 0
    hb = n_heads // SUBLANES
    cache = cache.reshape(cache.shape[:3] + (2 * hb, SUBLANES, dh))

    def cache_spec(j, r, kv):
        def imap(b, h, phys_ref, blk_ref):
            return (layer, phys_ref[((b * n_heads + h) * n_sel + j) * ppb + r], 0, kv * hb + h // SUBLANES, 0, 0)
        return pl.BlockSpec((None, None, page, None, SUBLANES, dh), imap)

    cache_specs = []
    for j in range(n_sel):
        for r in range(ppb):
            cache_specs += [cache_spec(j, r, 0), cache_spec(j, r, 1)]
    tok = lambda col0: pl.BlockSpec((1, 1, dh), lambda b, h, *_: (b, 0, col0 + h))
    grid_spec = pltpu.PrefetchScalarGridSpec(
        num_scalar_prefetch=2,
        grid=(B, n_heads),
        in_specs=[pl.BlockSpec(memory_space=pltpu.SMEM), tok(0), tok(0), tok(n_heads)] + cache_specs,
        out_specs=tok(0))
    return pl.pallas_call(
        functools.partial(_moba_attend_kernel, n_heads=n_heads, pos=pos, page=page, ppb=ppb),
        grid_spec=grid_spec,
        out_shape=jax.ShapeDtypeStruct((B, 1, n_heads * dh), BF16),
        compiler_params=_params("parallel", "arbitrary"),
        name="moba_decode_attend",
    )(phys.reshape(-1), blk.reshape(-1), _alibi_slopes(n_heads), q, kv_new, kv_new,
      *([cache] * (2 * n_sel * ppb)))


def _nsa_decode_compress_kernel(pt_ref, *refs, n_steps, page, n_groups):
    n_in = len(refs) - 5
    cache_refs = refs[:n_in]
    pe_ref, w1_ref, w2_ref, o_ref, buf_ref = refs[n_in:]
    st = pl.program_id(1)
    pps = n_in // 2
    for i in range(pps):
        row0 = pl.multiple_of((st * pps + i) * page, page)
        for c in range(2 * n_groups):
            buf_ref[c, pl.ds(row0, page), :] = cache_refs[2 * i + c // n_groups][:, c % n_groups, :]

    @pl.when(st == n_steps - 1)
    def _():
        dh = HEAD_DIM
        rows = buf_ref.shape[1] // CMP_STRIDE
        halves = CMP_LEN // CMP_STRIDE
        for c in range(2 * n_groups):
            which = c // n_groups
            parts = [None] * halves
            for l in range(CMP_STRIDE):
                x = buf_ref.at[c][pl.ds(l, rows, stride=CMP_STRIDE), :]
                for hf in range(halves):
                    ll = hf * CMP_STRIDE + l
                    d = _dot(x + pe_ref[which, ll:ll + 1, :], w1_ref[which, ll * dh:(ll + 1) * dh, :])
                    parts[hf] = d if parts[hf] is None else parts[hf] + d
            hid = parts[0]
            for hf in range(1, halves):
                hid = hid + pltpu.roll(parts[hf], rows - hf, axis=0)
            y = _dot(hid * jax.nn.sigmoid(hid), w2_ref[which])
            ridx = lax.broadcasted_iota(jnp.int32, y.shape, 0)
            o_ref[0, c] = jnp.where(ridx < rows - (halves - 1), y, 0.0)


def _nsa_decode_compress(cache, layer, page_table, pe, w1, w2, n_groups):
    B, n_pages = page_table.shape
    page = cache.shape[2]
    dh = HEAD_DIM
    past = n_pages * page
    rows = past // CMP_STRIDE
    hidden = w1.shape[-1]

    pps = DECODE_PAGES_PER_STEP
    assert n_pages % pps == 0
    n_steps = n_pages // pps

    def cache_spec(i, which):
        return pl.BlockSpec((None, None, page, None, n_groups, dh),
                            lambda b, st, pt: (layer, pt[b * n_pages + st * pps + i], 0, which, 0, 0))

    cache_specs = [cache_spec(i, which) for i in range(pps) for which in range(2)]
    grid_spec = pltpu.PrefetchScalarGridSpec(
        num_scalar_prefetch=1,
        grid=(B, n_steps),
        in_specs=cache_specs + [
                  pl.BlockSpec((None, 2, CMP_LEN, dh), lambda b, p, pt: (layer, 0, 0, 0)),
                  pl.BlockSpec((None, 2, CMP_LEN * dh, hidden), lambda b, p, pt: (layer, 0, 0, 0)),
                  pl.BlockSpec((None, 2, hidden, dh), lambda b, p, pt: (layer, 0, 0, 0))],
        out_specs=pl.BlockSpec((1, 2 * n_groups, rows, dh), lambda b, p, pt: (b, 0, 0, 0)),
        scratch_shapes=[pltpu.VMEM((2 * n_groups, past, dh), F32)])
    return pl.pallas_call(
        functools.partial(_nsa_decode_compress_kernel, n_steps=n_steps, page=page, n_groups=n_groups),
        grid_spec=grid_spec,
        out_shape=jax.ShapeDtypeStruct((B, 2 * n_groups, rows, dh), F32),
        compiler_params=_params("parallel", "arbitrary"),
        name="nsa_decode_compress",
    )(page_table.reshape(-1), *([cache] * (2 * pps)), pe, w1, w2)


def _split_heads(q, gs):
    dh = HEAD_DIM
    return jnp.concatenate([q[:, s * dh:(s + 1) * dh] for s in range(gs)], axis=0)


def _group_slopes(slopes_ref, g, gs):
    sub = lax.broadcasted_iota(jnp.int32, (gs, 1), 0)
    slope = jnp.zeros((gs, 1), F32)
    for s in range(gs):
        slope = jnp.where(sub == s, slopes_ref[g * gs + s], slope)
    return slope


def _nsa_select_kernel(slopes_ref, q_ref, kc_ref, vc_ref, ov_ref, ocmp_ref, idx_ref, *, gs, pos):
    g = pl.program_id(1)
    qs = _split_heads(q_ref[0], gs)
    slope = _group_slopes(slopes_ref, g, gs)
    ncp = kc_ref.shape[2]
    c_end = lax.broadcasted_iota(jnp.int32, (1, ncp), 1) * CMP_STRIDE + (CMP_LEN - 1)
    dist_c = pos - c_end
    cmask = dist_c >= 0
    s_c = jnp.where(cmask, _dot_nt(qs, kc_ref[0, 0]) * SCALE - slope * dist_c.astype(F32), NEG)
    e_c = jnp.where(cmask, jnp.exp(s_c - jnp.max(s_c, axis=-1, keepdims=True)), 0.0)
    den = jnp.sum(e_c, axis=-1, keepdims=True)
    p_c = e_c / jnp.where(den > 0.0, den, 1.0)
    ocmp_ref[0, 0] = _dot(p_c, vc_ref[0, 0])
    imp = jnp.sum(_dot(p_c, ov_ref[...]), axis=0, keepdims=True)
    idx = _topk_indices(imp, SEL_TOPK, idx_ref.shape[3])
    idx_ref[0, 0] = jnp.broadcast_to(idx, idx_ref.shape[2:])


def _nsa_decode_select(q, cmp_kv, n_heads, n_groups, pos):
    B = q.shape[0]
    G = n_groups
    gs = n_heads // G
    dh = HEAD_DIM
    R = cmp_kv.shape[2]
    cur = pos // SEL_BLOCK
    assert pos % SEL_BLOCK == 0 and cur >= SEL_TOPK
    c_start = np.arange(R) * CMP_STRIDE
    s_start = np.arange(cur) * SEL_BLOCK
    nc = (pos + 1 - CMP_LEN) // CMP_STRIDE + 1
    overlap = ((c_start[:, None] <= s_start[None, :] + SEL_BLOCK - 1)
               & (c_start[:, None] + CMP_LEN - 1 >= s_start[None, :])
               & (np.arange(R)[:, None] < nc))
    overlap = jnp.asarray(overlap.astype(np.float32), BF16)
    return pl.pallas_call(
        functools.partial(_nsa_select_kernel, gs=gs, pos=pos),
        grid=(B, G),
        in_specs=[pl.BlockSpec(memory_space=pltpu.SMEM),
                  pl.BlockSpec((1, 1, gs * dh), lambda b, g: (b, 0, g)),
                  pl.BlockSpec((1, 1, R, dh), lambda b, g: (b, g, 0, 0)),
                  pl.BlockSpec((1, 1, R, dh), lambda b, g: (b, G + g, 0, 0)),
                  pl.BlockSpec((R, cur), lambda b, g: (0, 0))],
        out_specs=[pl.BlockSpec((1, 1, gs, dh), lambda b, g: (b, g, 0, 0)),
                   pl.BlockSpec((1, 1, 8, 128), lambda b, g: (b, g, 0, 0))],
        out_shape=[jax.ShapeDtypeStruct((B, G, gs, dh), F32), jax.ShapeDtypeStruct((B, G, 8, 128), jnp.int32)],
        compiler_params=_params("parallel", "parallel"),
        name="nsa_decode_select",
    )(_alibi_slopes(n_heads), q, cmp_kv, cmp_kv, overlap)


def _pick_group(ref, g, n_groups):
    out = ref[:, 0, :]
    for gg in range(1, n_groups):
        out = jnp.where(g == gg, ref[:, gg, :], out)
    return out


def _nsa_attend_kernel(pool_blk_ref, blk_ref, slopes_ref, q_ref, gates_ref, ocmp_ref, ksn_ref, vsn_ref,
                       kwn_ref, vwn_ref, kw_ref, vw_ref, *rest, n_groups, gs, pos):
    cache_refs, o_ref = rest[:-1], rest[-1]
    b = pl.program_id(0)
    g = pl.program_id(1)
    dh = HEAD_DIM
    n_sel = len(cache_refs) // 2
    qs = _split_heads(q_ref[0], gs)
    slope = _group_slopes(slopes_ref, g, gs)

    def attend(k_past, v_past, dist, ok, k_new, v_new):
        s = jnp.where(ok, _dot_nt(qs, k_past) * SCALE - slope * dist.astype(F32), NEG)
        s_new = _dot_nt(qs, jnp.broadcast_to(k_new, (gs, dh)))[:, 0:1] * SCALE
        m = jnp.maximum(jnp.max(s, axis=-1, keepdims=True), s_new)
        p = jnp.exp(s - m)
        p_new = jnp.exp(s_new - m)
        den = jnp.sum(p, axis=-1, keepdims=True) + p_new
        acc = _dot(p, v_past) + _mx(p_new).astype(F32) * _mx(v_new).astype(F32)
        return acc / den

    lane = lax.broadcasted_iota(jnp.int32, (1, SEL_BLOCK), 1)
    dists = [pos - (blk_ref[(b * n_groups + g) * n_sel + j] * SEL_BLOCK + lane) for j in range(n_sel)]
    dist = jnp.concatenate(dists, axis=1)
    k_sel = jnp.concatenate([_pick_group(cache_refs[2 * j], g, n_groups) for j in range(n_sel)], axis=0)
    v_sel = jnp.concatenate([_pick_group(cache_refs[2 * j + 1], g, n_groups) for j in range(n_sel)], axis=0)
    o_sel = attend(k_sel, v_sel, dist, dist >= 0, ksn_ref[0], vsn_ref[0])

    n_buf = kw_ref.shape[0]
    dist_w = n_buf - lax.broadcasted_iota(jnp.int32, (1, n_buf), 1)
    o_win = attend(_pick_group(kw_ref, g, n_groups), _pick_group(vw_ref, g, n_groups), dist_w, dist_w < WINDOW,
                   kwn_ref[0], vwn_ref[0])

    gate = jax.nn.sigmoid(gates_ref[0, 0])
    o = gate[:, 0:1] * ocmp_ref[0, 0] + gate[:, 1:2] * o_sel + gate[:, 2:3] * o_win
    for s in range(gs):
        o_ref[0, :, s * dh:(s + 1) * dh] = o[s:s + 1].astype(o_ref.dtype)


def _nsa_decode_attend(q, gates, o_cmp, nsa_new, win_new, cache, win_state, layer, page_table, top_idx,
                       n_heads, n_groups, pos):
    B, n_pages = page_table.shape
    page = cache.shape[2]
    assert cache.shape[3] == 4
    G = n_groups
    gs = n_heads // G
    dh = HEAD_DIM
    per_page = page // SEL_BLOCK
    n_buf = win_state.shape[2]
    assert page % SEL_BLOCK == 0 and pos >= n_buf
    blk = top_idx[:, :, 0, :SEL_TOPK]
    phys = jnp.take_along_axis(page_table[:, None, :], (blk // per_page).reshape(B, 1, -1), axis=2)
    pool_blk = phys.reshape(B, G, SEL_TOPK) * per_page + blk % per_page
    n_sel = SEL_TOPK

    def cache_spec(j, kind):
        def imap(b, g, pool_blk_ref, blk_ref):
            pb = pool_blk_ref[(b * G + g) * n_sel + j]
            return (layer, pb // per_page, pb % per_page, kind, 0, 0)
        return pl.BlockSpec((None, None, SEL_BLOCK, None, G, dh), imap)

    cache_specs = []
    for j in range(n_sel):
        cache_specs += [cache_spec(j, 2), cache_spec(j, 3)]
    tok = lambda col0: pl.BlockSpec((1, 1, dh), lambda b, g, *_: (b, 0, col0 + g))
    win = lambda kv: pl.BlockSpec((None, None, n_buf, None, G, dh), lambda b, g, *_: (layer, b, 0, kv, 0, 0))
    grid_spec = pltpu.PrefetchScalarGridSpec(
        num_scalar_prefetch=2,
        grid=(B, G),
        in_specs=[pl.BlockSpec(memory_space=pltpu.SMEM),
                  pl.BlockSpec((1, 1, gs * dh), lambda b, g, *_: (b, 0, g)),
                  pl.BlockSpec((1, 1, gs, 3), lambda b, g, *_: (b, g, 0, 0)),
                  pl.BlockSpec((1, 1, gs, dh), lambda b, g, *_: (b, g, 0, 0)),
                  tok(2 * G), tok(3 * G), tok(0), tok(G), win(0), win(1)] + cache_specs,
        out_specs=pl.BlockSpec((1, 1, gs * dh), lambda b, g, *_: (b, 0, g)))
    return pl.pallas_call(
        functools.partial(_nsa_attend_kernel, n_groups=G, gs=gs, pos=pos),
        grid_spec=grid_spec,
        out_shape=jax.ShapeDtypeStruct((B, 1, n_heads * dh), BF16),
        compiler_params=_params("parallel", "arbitrary"),
        name="nsa_decode_attend",
    )(pool_blk.reshape(-1), blk.reshape(-1), _alibi_slopes(n_heads), q, gates, o_cmp,
      nsa_new, nsa_new, win_new, win_new, win_state, win_state, *([cache] * (2 * n_sel)))


def _trunk(x3, start, past, ln_g, ln_b, ffn_gate, ffn_up, ffn_down, attn_w_in, attn_w_out,
           cmp_pe, cmp_w1, cmp_w2, pool_w, pool_scale):
    B, T, D = x3.shape
    M = B * T
    depth = ffn_gate.shape[0]
    d_ff = ffn_gate.shape[-1]
    alpha = (2.0 * depth) ** 0.25
    decode = past is not None
    dh = HEAD_DIM
    n_heads = D // dh
    H_A = n_heads // 2
    NH = n_heads - H_A
    G = cmp_pe.shape[1]
    gs = NH // G
    moba_w, nsa_w, kvw = H_A * dh, NH * dh, G * dh
    tm = min(M, 1024)
    tm_ln = min(M, 256)
    pool_w_mx = pool_w.astype(MXU_DTYPE)
    if decode:
        assert T == 1
        cache_moba_kv, cache_nsa_kv, state_nsa_win, state_pool, page_table = past

    x = x3.reshape(M, D)
    xb = x.astype(BF16)
    new_moba, new_nsa, new_win, new_pool = [], [], [], []

    def ffn(x, xb, layer, sub, ln_idx):
        hidden = _ffn_up(xb, ffn_gate, ffn_up, layer, sub, tm)
        y = _matmul(hidden, ffn_down, (layer, sub), 0, D, F32, tm=tm, tn=256, kchunk=d_ff // 2, name="ffn_down")
        return _res_ln(y, x, ln_g[layer, ln_idx], ln_b[layer, ln_idx], alpha=alpha, coef=0.5, tm=tm_ln)

    for layer in range(depth):
        x, xb = ffn(x, xb, layer, 0, 0)
        if layer % 2 == 0:
            a = layer // 2
            proj = functools.partial(_matmul, xb, attn_w_in, (a,), tm=tm, name="in_proj")
            q_a = proj(0, moba_w, BF16, tn=256)
            moba_kv = proj(moba_w, 2 * moba_w, F32, tn=256)
            q_b = proj(3 * moba_w, nsa_w, BF16, tn=256)
            c0 = 3 * moba_w + nsa_w
            nsa_kv = proj(c0, 4 * kvw, F32, tn=256)
            win_kv = proj(c0 + 4 * kvw, 2 * kvw, F32, tn=256)
            gates = proj(c0 + 6 * kvw, 3 * NH, F32, tn=128)
            moba_new = moba_kv.reshape(B, T, 2, H_A, dh)
            nsa_new = nsa_kv.reshape(B, T, 4, G, dh)
            win_new = win_kv.reshape(B, T, 2, G, dh)
            q_a3, q_b3 = q_a.reshape(B, T, moba_w), q_b.reshape(B, T, nsa_w)
            moba_kv3, nsa_kv3, win_kv3 = (moba_kv.reshape(B, T, 2 * moba_w), nsa_kv.reshape(B, T, 4 * kvw),
                                          win_kv.reshape(B, T, 2 * kvw))
            if not decode:
                o_a = _moba_prompt(q_a3, moba_kv3, H_A)
                kv16 = nsa_new[:, :, 0:2].transpose(0, 2, 3, 1, 4).reshape(B, 2, G, T // 16, 16 * dh)
                cmp_kv = _compress_prompt(kv16, cmp_pe[a].reshape(2, 2, 16 * dh), cmp_w1, cmp_w2, a)
                gates_g = gates.reshape(B, T, 3, G, gs).transpose(0, 3, 1, 2, 4).reshape(B, G, T, 3 * gs)
                o_b = _nsa_prompt(q_b3, gates_g, cmp_kv, nsa_kv3, win_kv3, NH, G)
                win_state = win_new[:, -min(WINDOW, T):]
            else:
                top_a = _moba_decode_select(q_a3, cache_moba_kv, a, page_table, H_A)
                o_a = _moba_decode_attend(q_a3, moba_kv3, cache_moba_kv, a, page_table, top_a, H_A, start)
                cmp_kv = _nsa_decode_compress(cache_nsa_kv, a, page_table, cmp_pe, cmp_w1, cmp_w2, G)
                o_cmp, top_b = _nsa_decode_select(q_b3, cmp_kv, NH, G, start)
                gates_g = gates.reshape(B, 3, G, gs).transpose(0, 2, 3, 1)
                o_b = _nsa_decode_attend(q_b3, gates_g, o_cmp, nsa_kv3, win_kv3, cache_nsa_kv, state_nsa_win, a,
                                         page_table, top_b, NH, G, start)
                win_state = jnp.concatenate([state_nsa_win[a][:, T:], win_new], axis=1)
            mixed = jnp.concatenate([o_a, o_b], axis=-1).reshape(M, D)
            new_moba.append(moba_new)
            new_nsa.append(nsa_new)
            new_win.append(win_state)
            y = _matmul(mixed, attn_w_out, (a,), 0, D, F32, tm=tm, tn=256, name="out_proj")
            x, xb = _res_ln(y, x, ln_g[layer, 1], ln_b[layer, 1], alpha=alpha, coef=1.0, tm=tm_ln)
        else:
            p = layer // 2
            x3c = x.reshape(B, T, D)
            if decode:
                prev = jnp.concatenate([state_pool[p], x3c], axis=1)
                new_pool.append(prev[:, -POOL_BUF:])
                tt = T
            else:
                prev = x3c
                new_pool.append(x3c[:, -POOL_BUF:])
                tt = 256
            xo, xbo = _pool_mixer_ln(x3c, prev, pool_w_mx[p], pool_scale[p], ln_g[layer, 1], ln_b[layer, 1],
                                     alpha=alpha, start=start, tt=tt, decode=decode)
            x, xb = xo.reshape(M, D), xbo.reshape(M, D)
        x, xb = ffn(x, xb, layer, 1, 2)
    return (x.reshape(B, T, D), jnp.stack(new_moba), jnp.stack(new_nsa), jnp.stack(new_win),
            jnp.stack(new_pool))


def kernel(x_prompt, x_sample, cache_moba_kv, cache_nsa_kv, state_nsa_win, state_pool, page_table,
           ln_g, ln_b, ffn_gate, ffn_up, ffn_down, attn_w_in, attn_w_out, cmp_pe, cmp_w1, cmp_w2,
           pool_w, pool_scale):
    weights = (ln_g, ln_b, ffn_gate, ffn_up, ffn_down, attn_w_in, attn_w_out, cmp_pe, cmp_w1, cmp_w2,
               pool_w, pool_scale)
    past_len = page_table.shape[1] * cache_moba_kv.shape[2]
    y_p, moba_p, nsa_p, win_p, pool_p = _trunk(x_prompt, 0, None, *weights)
    y_s, moba_s, nsa_s, win_s, pool_s = _trunk(
        x_sample, past_len, (cache_moba_kv, cache_nsa_kv, state_nsa_win, state_pool, page_table), *weights)
    return (y_p, y_s, moba_p, nsa_p, win_p, pool_p, moba_s, nsa_s, win_s, pool_s)
```

```python
import functools

import jax
import jax.numpy as jnp
import numpy as np
from jax import lax
from jax.experimental import pallas as pl
from jax.experimental.pallas import tpu as pltpu

F32 = jnp.float32
BF16 = jnp.bfloat16
MXU_DTYPE = BF16

HEAD_DIM = 128
MOBA_BLOCK = 256
MOBA_TOPK = 3
CMP_LEN = 32
CMP_STRIDE = 16
SEL_BLOCK = 64
SEL_TOPK = 15
WINDOW = 512
POOL_WINDOWS = (2, 4, 8, 16)
POOL_BUF = max(POOL_WINDOWS) - 1
LN_EPS = 1e-5
SCALE = HEAD_DIM ** -0.5
NEG = -1e30
LOWEST = -3e38
NSA_Q_TILE = 256
NSA_KEY_CHUNK = 1024
MOBA_KEY_CHUNK = 1024
MOBA_HEADS_PER_STEP = 2
SUBLANES = 8
DECODE_PAGES_PER_STEP = 8
FFN_ROW_TILE = 1024

V7X_VMEM_BYTES = 64 * 1024 * 1024
VMEM_LIMIT = V7X_VMEM_BYTES - 8 * 1024 * 1024


def _params(*sem):
    return pltpu.CompilerParams(dimension_semantics=sem, vmem_limit_bytes=VMEM_LIMIT)


def _mx(a):
    return a.astype(MXU_DTYPE)


def _dot(a, b):
    return jnp.dot(_mx(a), _mx(b), preferred_element_type=F32)


def _dot_tn(a, b):
    return lax.dot_general(_mx(a), _mx(b), (((0,), (0,)), ((), ())), preferred_element_type=F32)


def _dot_nt(a, b):
    return lax.dot_general(_mx(a), _mx(b), (((1,), (1,)), ((), ())), preferred_element_type=F32)


def _alibi_slopes(n):
    return jnp.asarray(np.exp2(-8.0 * np.arange(1, n + 1) / n), dtype=F32)


def _layernorm_rows(z, g, b):
    mu = jnp.mean(z, axis=-1, keepdims=True)
    zc = z - mu
    var = jnp.mean(zc * zc, axis=-1, keepdims=True)
    return zc * lax.rsqrt(var + LN_EPS) * g + b


def _swiglu_tile(x, wg, wu, dtype):
    g = _dot(x, wg)
    u = _dot(x, wu)
    return (g * jax.nn.sigmoid(g) * u).astype(dtype)


def _ffn_up_kernel(x_ref, xs_ref, wg_ref, wu_ref, o_ref, os_ref):
    wg, wu = _mx(wg_ref[...]), _mx(wu_ref[...])
    o_ref[...] = _swiglu_tile(x_ref[...], wg, wu, o_ref.dtype)

    @pl.when(pl.program_id(1) == 0)
    def _():
        os_ref[...] = _swiglu_tile(xs_ref[...], wg, wu, os_ref.dtype)


def _ffn_up(xb, xsb, w_gate, w_up, layer, sub, tm, tn=256):
    M, D = xb.shape
    Ms = xsb.shape[0]
    F = w_gate.shape[-1]
    assert M % tm == 0 and F % tn == 0
    wspec = pl.BlockSpec((None, None, D, tn), lambda j, i: (layer, sub, 0, j))
    return pl.pallas_call(
        _ffn_up_kernel,
        grid=(F // tn, M // tm),
        in_specs=[pl.BlockSpec((tm, D), lambda j, i: (i, 0)), pl.BlockSpec((Ms, D), lambda j, i: (0, 0)),
                  wspec, wspec],
        out_specs=[pl.BlockSpec((tm, tn), lambda j, i: (i, j)), pl.BlockSpec((Ms, tn), lambda j, i: (0, j))],
        out_shape=[jax.ShapeDtypeStruct((M, F), BF16), jax.ShapeDtypeStruct((Ms, F), BF16)],
        compiler_params=_params("parallel", "arbitrary"),
        name="ffn_up",
    )(xb, xsb, w_gate, w_up)


def _matmul_kernel(x_ref, w_ref, o_ref, *, kchunk):
    K = x_ref.shape[1]
    acc = None
    for k0 in range(0, K, kchunk):
        part = _dot(x_ref[:, k0:k0 + kchunk], w_ref[k0:k0 + kchunk, :])
        acc = part if acc is None else acc + part
    o_ref[...] = acc[:, :o_ref.shape[1]].astype(o_ref.dtype)


def _matmul(xb, w, widx, col0, ncols, out_dtype, *, tm, tn, kchunk=None, name):
    M, K = xb.shape
    assert M % tm == 0 and col0 % tn == 0
    kchunk = K if kchunk is None else kchunk
    assert K % kchunk == 0
    if ncols % tn == 0:
        nj, out_tn = ncols // tn, tn
    else:
        assert ncols < tn
        nj, out_tn = 1, ncols
    j0 = col0 // tn
    lead = (None,) * len(widx)
    x_bytes = tm * K * xb.dtype.itemsize
    x_mode = {"pipeline_mode": pl.Buffered(1)} if 4 * x_bytes > VMEM_LIMIT else {}
    x_spec = pl.BlockSpec((tm, K), lambda i, j: (i, 0), **x_mode)
    return pl.pallas_call(
        functools.partial(_matmul_kernel, kchunk=kchunk),
        grid=(M // tm, nj),
        in_specs=[x_spec, pl.BlockSpec(lead + (K, tn), lambda i, j: tuple(widx) + (0, j0 + j))],
        out_specs=pl.BlockSpec((tm, out_tn), lambda i, j: (i, j)),
        out_shape=jax.ShapeDtypeStruct((M, ncols), out_dtype),
        compiler_params=_params("parallel", "arbitrary"),
        name=name,
    )(xb, w)


def _res_ln_kernel(y_ref, res_ref, g_ref, b_ref, o_ref, ob_ref, *, alpha, coef):
    z = alpha * res_ref[...] + coef * y_ref[...]
    y = _layernorm_rows(z, g_ref[...], b_ref[...])
    o_ref[...] = y
    ob_ref[...] = y.astype(ob_ref.dtype)


def _res_ln(y, res, g, b, *, alpha, coef, tm):
    M, D = y.shape
    assert M % tm == 0
    row = pl.BlockSpec((1, D), lambda i: (0, 0))
    tile = pl.BlockSpec((tm, D), lambda i: (i, 0))
    return pl.pallas_call(
        functools.partial(_res_ln_kernel, alpha=alpha, coef=coef),
        grid=(M // tm,),
        in_specs=[tile, tile, row, row],
        out_specs=[tile, tile],
        out_shape=[jax.ShapeDtypeStruct((M, D), F32), jax.ShapeDtypeStruct((M, D), BF16)],
        compiler_params=_params("parallel"),
        name="res_ln",
    )(y, res, g.reshape(1, D), b.reshape(1, D))


def _pool_kernel(x_ref, prev_ref, w_ref, sc_ref, g_ref, b_ref, o_ref, ob_ref, *,
                 alpha, start, tt, decode, group):
    ti = pl.program_id(1)
    zs = []
    for gi, win in enumerate(POOL_WINDOWS):
        c0, c1 = gi * group, (gi + 1) * group
        prev = prev_ref[0, :, c0:c1]
        if decode:
            xg = prev[POOL_BUF:POOL_BUF + 1]
            wsum = jnp.sum(prev[POOL_BUF + 1 - win:], axis=0, keepdims=True)
            cnt = jnp.full((1, 1), float(min(win, start + 1)), F32)
        else:
            xg = x_ref[0, :, c0:c1]
            prev = jnp.where(ti == 0, 0.0, prev)
            s = jnp.concatenate([prev, xg], axis=0)
            span = 1
            while span < win:
                s = s + pltpu.roll(s, span, axis=0)
                span *= 2
            wsum = s[POOL_BUF + 1:]
            pos = start + ti * tt + lax.broadcasted_iota(jnp.int32, (tt, 1), 0)
            cnt = jnp.minimum(win, pos + 1).astype(F32)
        pooled = wsum / cnt - xg
        y = _dot(pooled, w_ref[gi]) * sc_ref[:, c0:c1]
        zs.append(alpha * xg + y)
    d = group * len(POOL_WINDOWS)
    mu = sum(jnp.sum(z, axis=-1, keepdims=True) for z in zs) / d
    var = sum(jnp.sum((z - mu) * (z - mu), axis=-1, keepdims=True) for z in zs) / d
    inv = lax.rsqrt(var + LN_EPS)
    for gi, z in enumerate(zs):
        c0, c1 = gi * group, (gi + 1) * group
        y = (z - mu) * inv * g_ref[:, c0:c1] + b_ref[:, c0:c1]
        o_ref[0, :, c0:c1] = y
        ob_ref[0, :, c0:c1] = y.astype(ob_ref.dtype)


def _pool_mixer_ln(x3, prev3, w_pool, scale, g, b, *, alpha, start, tt, decode):
    B, T, D = x3.shape
    group = D // len(POOL_WINDOWS)
    assert T % tt == 0 and (decode or tt % 16 == 0)
    per = tt // 16 if not decode else 0
    if decode:
        prev_map = lambda bi, ti: (bi, 0, 0)
    else:
        prev_map = lambda bi, ti: (bi, jnp.maximum(ti * per - 1, 0), 0)
    row = pl.BlockSpec((1, D), lambda bi, ti: (0, 0))
    tile = pl.BlockSpec((1, tt, D), lambda bi, ti: (bi, ti, 0))
    return pl.pallas_call(
        functools.partial(_pool_kernel, alpha=alpha, start=start, tt=tt, decode=decode, group=group),
        grid=(B, T // tt),
        in_specs=[tile, pl.BlockSpec((1, 16, D), prev_map),
                  pl.BlockSpec(w_pool.shape, lambda bi, ti: (0, 0, 0)), row, row, row],
        out_specs=[tile, tile],
        out_shape=[jax.ShapeDtypeStruct((B, T, D), F32), jax.ShapeDtypeStruct((B, T, D), BF16)],
        compiler_params=_params("parallel", "arbitrary"),
        name="pool_mixer_ln",
    )(x3, prev3, w_pool, scale.reshape(1, D), g.reshape(1, D), b.reshape(1, D))


def _topk_mask_t(score_t, row, n_valid, k):
    n = score_t.shape[0]
    masked = jnp.where(row < n_valid, score_t, NEG)
    rank = jnp.zeros(score_t.shape, jnp.int32)
    for j in range(n):
        other = masked[j:j + 1, :]
        ahead = (other > masked) | ((other == masked) & (j < row))
        rank = rank + ahead.astype(jnp.int32)
    return (rank < k) & (row < n_valid)


def _online_update(m_ref, l_ref, acc_ref, s, v):
    m_old = m_ref[...]
    m_new = jnp.maximum(m_old, jnp.max(s, axis=-1, keepdims=True))
    p = jnp.exp(s - m_new)
    a = jnp.exp(m_old - m_new)
    l_ref[...] = a * l_ref[...] + jnp.sum(p, axis=-1, keepdims=True)
    acc_ref[...] = a * acc_ref[...] + _dot(p, v)
    m_ref[...] = m_new


def _moba_kernel(slopes_ref, q_ref, k_ref, v_ref, ex_ref, o_ref, m_ref, l_ref, acc_ref, kmean_ref, *, nb, hpg):
    hg = pl.program_id(1)
    qi = pl.program_id(2)
    blk = MOBA_BLOCK
    dh = HEAD_DIM
    T = nb * blk
    kc = min(T, MOBA_KEY_CHUNK)
    heads = [(e * dh, (e + 1) * dh) for e in range(hpg)]
    slopes = [slopes_ref[hg * hpg + e] for e in range(hpg)]
    qs = [q_ref[0, :, c0:c1] for c0, c1 in heads]
    row = lax.broadcasted_iota(jnp.int32, (nb, blk), 0)

    @pl.when(qi == 0)
    def _():
        for e, (c0, c1) in enumerate(heads):
            kmean_ref[e] = jnp.concatenate(
                [jnp.sum(k_ref[0, n * blk:(n + 1) * blk, c0:c1], axis=0, keepdims=True) for n in range(nb)],
                axis=0) * (1.0 / blk)

    sel_keys = []
    for e, (c0, c1) in enumerate(heads):
        gate_t = _dot_nt(kmean_ref[e], qs[e])
        sel_t = _topk_mask_t(gate_t, row, qi, MOBA_TOPK) | (row == qi)
        sel_keys.append(_dot_tn(sel_t.astype(F32), ex_ref[...]))
    qpos = qi * blk + lax.broadcasted_iota(jnp.int32, (blk, 1), 0)

    m_ref[...] = jnp.full(m_ref.shape, NEG, F32)
    l_ref[...] = jnp.zeros(l_ref.shape, F32)
    acc_ref[...] = jnp.zeros(acc_ref.shape, F32)
    for c in reversed(range(T // kc)):
        @pl.when(c * kc <= qi * blk)
        def _():
            dist = qpos - (c * kc + lax.broadcasted_iota(jnp.int32, (1, kc), 1))
            causal = dist >= 0
            distf = dist.astype(F32)
            for e, (c0, c1) in enumerate(heads):
                ok = (sel_keys[e][:, c * kc:(c + 1) * kc] > 0.5) & causal
                s = (_dot_nt(qs[e], k_ref[0, c * kc:(c + 1) * kc, c0:c1]) * SCALE - slopes[e] * distf
                     + jnp.where(ok, 0.0, NEG))
                _online_update(m_ref.at[e], l_ref.at[e], acc_ref.at[e], s, v_ref[0, c * kc:(c + 1) * kc, c0:c1])
    for e, (c0, c1) in enumerate(heads):
        o_ref[0, :, c0:c1] = (acc_ref[e] / l_ref[e]).astype(o_ref.dtype)


def _moba_prompt(q, kv, n_heads):
    B, T, _ = q.shape
    hpg = MOBA_HEADS_PER_STEP
    assert T % MOBA_BLOCK == 0 and T % min(T, MOBA_KEY_CHUNK) == 0 and MOBA_KEY_CHUNK % MOBA_BLOCK == 0
    assert n_heads % hpg == 0
    nb = T // MOBA_BLOCK
    dh = HEAD_DIM
    w = hpg * dh
    expand = (np.arange(T)[None, :] // MOBA_BLOCK) == np.arange(nb)[:, None]
    expand = jnp.asarray(expand.astype(np.float32), BF16)
    return pl.pallas_call(
        functools.partial(_moba_kernel, nb=nb, hpg=hpg),
        grid=(B, n_heads // hpg, nb),
        in_specs=[pl.BlockSpec(memory_space=pltpu.SMEM),
                  pl.BlockSpec((1, MOBA_BLOCK, w), lambda b, h, i: (b, i, h)),
                  pl.BlockSpec((1, T, w), lambda b, h, i: (b, 0, h)),
                  pl.BlockSpec((1, T, w), lambda b, h, i: (b, 0, n_heads // hpg + h)),
                  pl.BlockSpec((nb, T), lambda b, h, i: (0, 0))],
        out_specs=pl.BlockSpec((1, MOBA_BLOCK, w), lambda b, h, i: (b, i, h)),
        out_shape=jax.ShapeDtypeStruct((B, T, n_heads * dh), BF16),
        scratch_shapes=[pltpu.VMEM((hpg, MOBA_BLOCK, 1), F32), pltpu.VMEM((hpg, MOBA_BLOCK, 1), F32),
                        pltpu.VMEM((hpg, MOBA_BLOCK, dh), F32), pltpu.VMEM((hpg, nb, dh), F32)],
        compiler_params=_params("parallel", "parallel", "arbitrary"),
        name="moba_prompt",
    )(_alibi_slopes(n_heads), q, kv, kv, expand)


def _compress_kernel(x_ref, pe_ref, w1_ref, w2_ref, o_ref, *, nc):
    x = x_ref[0, 0, 0]
    half = x.shape[1]
    lo = _dot(x + pe_ref[0, 0:1, :], w1_ref[:half, :])
    hi = _dot(x + pe_ref[0, 1:2, :], w1_ref[half:, :])
    rows = x.shape[0]
    hid = lo + pltpu.roll(hi, rows - 1, axis=0)
    y = _dot(hid * jax.nn.sigmoid(hid), w2_ref[...])
    ridx = lax.broadcasted_iota(jnp.int32, y.shape, 0)
    o_ref[0, 0, 0] = jnp.where(ridx < nc, y, 0.0)


def _compress_prompt(kv16, pe, w1, w2, layer):
    B, two, G, R, W = kv16.shape
    nc = R - 1
    hidden = w1.shape[-1]
    return pl.pallas_call(
        functools.partial(_compress_kernel, nc=nc),
        grid=(B, two, G),
        in_specs=[pl.BlockSpec((1, 1, 1, R, W), lambda b, c, g: (b, c, g, 0, 0)),
                  pl.BlockSpec((1, 2, W), lambda b, c, g: (c, 0, 0)),
                  pl.BlockSpec((None, None, 2 * W, hidden), lambda b, c, g: (layer, c, 0, 0)),
                  pl.BlockSpec((None, None, hidden, HEAD_DIM), lambda b, c, g: (layer, c, 0, 0))],
        out_specs=pl.BlockSpec((1, 1, 1, R, HEAD_DIM), lambda b, c, g: (b, c, g, 0, 0)),
        out_shape=jax.ShapeDtypeStruct((B, two, G, R, HEAD_DIM), F32),
        compiler_params=_params("parallel", "parallel", "parallel"),
        name="nsa_compress",
    )(kv16, pe, w1, w2)


def _nsa_kernel(slopes_ref, q_ref, gates_ref, kc_ref, vc_ref, ks_ref, vs_ref, kw_ref, vw_ref,
                ov_ref, ex_ref, o_ref, m_ref, l_ref, acc_ref, *, T, gs):
    g = pl.program_id(1)
    qi = pl.program_id(2)
    tq = NSA_Q_TILE
    dh = HEAD_DIM
    q = q_ref[0]
    qs = jnp.concatenate([q[:, s * dh:(s + 1) * dh] for s in range(gs)], axis=0)
    slopes = [slopes_ref[g * gs + s] for s in range(gs)]
    heads = [(s * tq, (s + 1) * tq) for s in range(gs)]
    tpos = qi * tq + lax.broadcasted_iota(jnp.int32, (tq, 1), 0)

    ncp = kc_ref.shape[3]
    c_end = lax.broadcasted_iota(jnp.int32, (1, ncp), 1) * CMP_STRIDE + (CMP_LEN - 1)
    dist_c = tpos - c_end
    cmask = dist_c >= 0
    bias_c = jnp.where(cmask, 0.0, NEG)
    distf_c = dist_c.astype(F32)
    s_all = _dot_nt(qs, kc_ref[0, 0, 0])
    p_list = []
    for s, (r0, r1) in enumerate(heads):
        sc = s_all[r0:r1] * SCALE - slopes[s] * distf_c + bias_c
        e = jnp.where(cmask, jnp.exp(sc - jnp.max(sc, axis=-1, keepdims=True)), 0.0)
        den = jnp.sum(e, axis=-1, keepdims=True)
        p_list.append(e / jnp.where(den > 0.0, den, 1.0))
    p_c = _mx(jnp.concatenate(p_list, axis=0))
    o_cmp = _dot(p_c, vc_ref[0, 0, 0])

    imp_cols = _dot_nt(ov_ref[...], p_c)
    imp_t = imp_cols[:, 0:tq]
    for r0, r1 in heads[1:]:
        imp_t = imp_t + imp_cols[:, r0:r1]
    ns = imp_t.shape[0]
    rowj = lax.broadcasted_iota(jnp.int32, (ns, tq), 0)
    cur_t = (qi * tq + lax.broadcasted_iota(jnp.int32, (1, tq), 1)) // SEL_BLOCK
    sel_t = _topk_mask_t(imp_t, rowj, cur_t, SEL_TOPK).astype(F32)
    sel_keys = _dot_tn(sel_t, ex_ref[...])

    m_ref[...] = jnp.full(m_ref.shape, NEG, F32)
    l_ref[...] = jnp.zeros(l_ref.shape, F32)
    acc_ref[...] = jnp.zeros(acc_ref.shape, F32)
    kc = NSA_KEY_CHUNK
    for c in reversed(range(T // kc)):
        @pl.when(c * kc < (qi + 1) * tq)
        def _():
            kpos = c * kc + lax.broadcasted_iota(jnp.int32, (1, kc), 1)
            dist = tpos - kpos
            own = (kpos // SEL_BLOCK) == (tpos // SEL_BLOCK)
            ok = ((sel_keys[:, c * kc:(c + 1) * kc] > 0.5) | own) & (dist >= 0)
            bias = jnp.where(ok, 0.0, NEG)
            distf = dist.astype(F32)
            s_all = _dot_nt(qs, ks_ref[0, c * kc:(c + 1) * kc, :])
            m_all = m_ref[...]
            ps, maxes, sums = [], [], []
            for s, (r0, r1) in enumerate(heads):
                sc = s_all[r0:r1] * SCALE - slopes[s] * distf + bias
                m_new = jnp.maximum(m_all[r0:r1], jnp.max(sc, axis=-1, keepdims=True))
                p = jnp.exp(sc - m_new)
                maxes.append(m_new)
                sums.append(jnp.sum(p, axis=-1, keepdims=True))
                ps.append(_mx(p))
            m_new = jnp.concatenate(maxes, axis=0)
            a = jnp.exp(m_all - m_new)
            pv = _dot(jnp.concatenate(ps, axis=0), vs_ref[0, c * kc:(c + 1) * kc, :])
            l_ref[...] = a * l_ref[...] + jnp.concatenate(sums, axis=0)
            acc_ref[...] = a * acc_ref[...] + pv
            m_ref[...] = m_new
    o_sel = acc_ref[...] / l_ref[...]

    span = WINDOW + tq
    w0 = pl.multiple_of(jnp.maximum(qi * tq - WINDOW, 0), tq)
    kpos = w0 + lax.broadcasted_iota(jnp.int32, (1, span), 1)
    dist = tpos - kpos
    bias = jnp.where((dist >= 0) & (dist < WINDOW), 0.0, NEG)
    distf = dist.astype(F32)
    s_all = _dot_nt(qs, kw_ref[0, pl.ds(w0, span), :])
    ps = []
    for s, (r0, r1) in enumerate(heads):
        sc = s_all[r0:r1] * SCALE - slopes[s] * distf + bias
        e = jnp.exp(sc - jnp.max(sc, axis=-1, keepdims=True))
        ps.append(_mx(e / jnp.sum(e, axis=-1, keepdims=True)))
    o_win = _dot(jnp.concatenate(ps, axis=0), vw_ref[0, pl.ds(w0, span), :])

    gate = jax.nn.sigmoid(gates_ref[0, 0])
    for s, (r0, r1) in enumerate(heads):
        o = (gate[:, s:s + 1] * o_cmp[r0:r1] + gate[:, gs + s:gs + s + 1] * o_sel[r0:r1]
             + gate[:, 2 * gs + s:2 * gs + s + 1] * o_win[r0:r1])
        o_ref[0, :, s * dh:(s + 1) * dh] = o.astype(o_ref.dtype)


def _nsa_tables(T, rows_cmp):
    ns = -(-T // SEL_BLOCK)
    c_start = np.arange(rows_cmp) * CMP_STRIDE
    s_start = np.arange(ns) * SEL_BLOCK
    nc = (T - CMP_LEN) // CMP_STRIDE + 1
    overlap = ((c_start[:, None] <= s_start[None, :] + SEL_BLOCK - 1)
               & (c_start[:, None] + CMP_LEN - 1 >= s_start[None, :])
               & (np.arange(rows_cmp)[:, None] < nc))
    expand = (np.arange(T)[None, :] // SEL_BLOCK) == np.arange(ns)[:, None]
    return jnp.asarray(overlap.T.astype(np.float32), BF16), jnp.asarray(expand.astype(np.float32), BF16)


def _nsa_prompt(q, gates, cmp_kv, nsa_kv, win_kv, n_heads, n_groups):
    B, T, _ = q.shape
    G = n_groups
    gs = n_heads // G
    dh = HEAD_DIM
    tq = NSA_Q_TILE
    assert T % NSA_KEY_CHUNK == 0 and T >= WINDOW + tq and T % tq == 0
    R = cmp_kv.shape[3]
    overlap_t, expand = _nsa_tables(T, R)
    ns = overlap_t.shape[0]
    rows = gs * tq
    seq = lambda col: pl.BlockSpec((1, T, dh), lambda b, g, i: (b, 0, col(g)))
    return pl.pallas_call(
        functools.partial(_nsa_kernel, T=T, gs=gs),
        grid=(B, G, T // tq),
        in_specs=[pl.BlockSpec(memory_space=pltpu.SMEM),
                  pl.BlockSpec((1, tq, gs * dh), lambda b, g, i: (b, i, g)),
                  pl.BlockSpec((1, 1, tq, 3 * gs), lambda b, g, i: (b, g, i, 0)),
                  pl.BlockSpec((1, 1, 1, R, dh), lambda b, g, i: (b, 0, g, 0, 0)),
                  pl.BlockSpec((1, 1, 1, R, dh), lambda b, g, i: (b, 1, g, 0, 0)),
                  seq(lambda g: 2 * G + g), seq(lambda g: 3 * G + g),
                  seq(lambda g: g), seq(lambda g: G + g),
                  pl.BlockSpec((ns, R), lambda b, g, i: (0, 0)),
                  pl.BlockSpec((ns, T), lambda b, g, i: (0, 0))],
        out_specs=pl.BlockSpec((1, tq, gs * dh), lambda b, g, i: (b, i, g)),
        out_shape=jax.ShapeDtypeStruct((B, T, n_heads * dh), BF16),
        scratch_shapes=[pltpu.VMEM((rows, 1), F32), pltpu.VMEM((rows, 1), F32), pltpu.VMEM((rows, dh), F32)],
        compiler_params=_params("parallel", "parallel", "arbitrary"),
        name="nsa_prompt",
    )(_alibi_slopes(n_heads), q, gates, cmp_kv, cmp_kv, nsa_kv, nsa_kv, win_kv, win_kv, overlap_t, expand)


def _first_argmax(score, lane):
    mx = jnp.max(score, axis=-1, keepdims=True)
    return jnp.min(jnp.where(score == mx, lane, float(score.shape[1])), axis=-1, keepdims=True)


def _topk_indices(score, k, width):
    lane = lax.broadcasted_iota(jnp.int32, score.shape, 1).astype(F32)
    out_lane = lax.broadcasted_iota(jnp.int32, (score.shape[0], width), 1)
    out = jnp.zeros((score.shape[0], width), jnp.int32)
    for j in range(k):
        idx = _first_argmax(score, lane)
        out = jnp.where(out_lane == j, idx.astype(jnp.int32), out)
        score = jnp.where(lane == idx, LOWEST, score)
    return out


def _moba_select_kernel(pt_ref, q_ref, *rest, n_heads, n_steps, pages_per_block):
    cache_refs, (idx_ref, ksum_ref) = rest[:-2], rest[-2:]
    st = pl.program_id(1)
    blocks_per_step = len(cache_refs) // pages_per_block
    for i in range(blocks_per_step):
        total = None
        for r in range(pages_per_block):
            part = jnp.sum(cache_refs[i * pages_per_block + r][...], axis=0, keepdims=True)
            total = part if total is None else total + part
        ksum_ref[pl.ds(st * blocks_per_step + i, 1)] = total

    @pl.when(st == n_steps - 1)
    def _():
        dh = HEAD_DIM
        q = q_ref[0]
        gate = jnp.concatenate(
            [_dot_nt(q[:, h * dh:(h + 1) * dh], ksum_ref[:, h, :] * (1.0 / MOBA_BLOCK)) for h in range(n_heads)],
            axis=0)
        idx_ref[0] = _topk_indices(gate, MOBA_TOPK, idx_ref.shape[2])


def _moba_decode_select(q, cache, layer, page_table, n_heads):
    B, n_pages = page_table.shape
    page = cache.shape[2]
    width = n_heads * HEAD_DIM
    ppb = MOBA_BLOCK // page
    pps = DECODE_PAGES_PER_STEP
    assert MOBA_BLOCK % page == 0 and pps % ppb == 0 and n_pages % pps == 0 and n_pages // ppb >= MOBA_TOPK
    n_steps = n_pages // pps

    def cache_spec(i):
        return pl.BlockSpec((None, None, page, None, n_heads, HEAD_DIM),
                            lambda b, st, pt: (layer, pt[b * n_pages + st * pps + i], 0, 0, 0, 0))

    grid_spec = pltpu.PrefetchScalarGridSpec(
        num_scalar_prefetch=1,
        grid=(B, n_steps),
        in_specs=[pl.BlockSpec((1, 1, width), lambda b, st, pt: (b, 0, 0))] + [cache_spec(i) for i in range(pps)],
        out_specs=pl.BlockSpec((1, n_heads, 128), lambda b, st, pt: (b, 0, 0)),
        scratch_shapes=[pltpu.VMEM((n_pages // ppb, n_heads, HEAD_DIM), F32)])
    return pl.pallas_call(
        functools.partial(_moba_select_kernel, n_heads=n_heads, n_steps=n_steps, pages_per_block=ppb),
        grid_spec=grid_spec,
        out_shape=jax.ShapeDtypeStruct((B, n_heads, 128), jnp.int32),
        compiler_params=_params("parallel", "arbitrary"),
        name="moba_decode_select",
    )(page_table.reshape(-1), q, *([cache] * pps))


def _moba_attend_kernel(phys_ref, blk_ref, slopes_ref, q_ref, kn_ref, vn_ref, *rest, n_heads, pos, page, ppb):
    cache_refs, o_ref = rest[:-1], rest[-1]
    b = pl.program_id(0)
    h = pl.program_id(1)
    slope = slopes_ref[h]
    q = q_ref[0]
    n_sel = len(cache_refs) // (2 * ppb)
    rows = page * SUBLANES
    ridx = lax.broadcasted_iota(jnp.int32, (1, rows), 1)
    mine = (ridx % SUBLANES) == (h % SUBLANES)
    scores, values = [], []
    for j in range(n_sel):
        blk = blk_ref[(b * n_heads + h) * n_sel + j]
        for r in range(ppb):
            k_ref, v_ref = cache_refs[2 * (j * ppb + r)], cache_refs[2 * (j * ppb + r) + 1]
            dist = pos - (blk * MOBA_BLOCK + r * page + ridx // SUBLANES)
            s = _dot_nt(q, k_ref[...].reshape(rows, HEAD_DIM)) * SCALE - slope * dist.astype(F32)
            scores.append(jnp.where(mine, s, NEG))
            values.append(v_ref)
    s_new = jnp.sum(_mx(q).astype(F32) * _mx(kn_ref[0]).astype(F32), axis=-1, keepdims=True) * SCALE
    m = s_new
    for s in scores:
        m = jnp.maximum(m, jnp.max(s, axis=-1, keepdims=True))
    p_new = jnp.exp(s_new - m)
    den = p_new
    acc = _mx(p_new).astype(F32) * _mx(vn_ref[0]).astype(F32)
    for s, v_ref in zip(scores, values):
        p = jnp.exp(s - m)
        den = den + jnp.sum(p, axis=-1, keepdims=True)
        acc = acc + _dot(p, v_ref[...].reshape(rows, HEAD_DIM))
    o_ref[0] = (acc / den).astype(o_ref.dtype)


def _moba_decode_attend(q, kv_new, cache, layer, page_table, top_idx, n_heads, pos):
    B, n_pages = page_table.shape
    page = cache.shape[2]
    dh = HEAD_DIM
    ppb = MOBA_BLOCK // page
    blk = top_idx[:, :, :MOBA_TOPK]
    pages = blk[..., None] * ppb + jnp.arange(ppb, dtype=jnp.int32)
    phys = jnp.take_along_axis(page_table[:, None, :], pages.reshape(B, 1, -1), axis=2)
    n_sel = MOBA_TOPK

    assert n_heads % SUBLANES == 0
    hb = n_heads // SUBLANES
    cache = cache.reshape(cache.shape[:3] + (2 * hb, SUBLANES, dh))

    def cache_spec(j, r, kv):
        def imap(b, h, phys_ref, blk_ref):
            return (layer, phys_ref[((b * n_heads + h) * n_sel + j) * ppb + r], 0, kv * hb + h // SUBLANES, 0, 0)
        return pl.BlockSpec((None, None, page, None, SUBLANES, dh), imap)

    cache_specs = []
    for j in range(n_sel):
        for r in range(ppb):
            cache_specs += [cache_spec(j, r, 0), cache_spec(j, r, 1)]
    tok = lambda col0: pl.BlockSpec((1, 1, dh), lambda b, h, *_: (b, 0, col0 + h))
    grid_spec = pltpu.PrefetchScalarGridSpec(
        num_scalar_prefetch=2,
        grid=(B, n_heads),
        in_specs=[pl.BlockSpec(memory_space=pltpu.SMEM), tok(0), tok(0), tok(n_heads)] + cache_specs,
        out_specs=tok(0))
    return pl.pallas_call(
        functools.partial(_moba_attend_kernel, n_heads=n_heads, pos=pos, page=page, ppb=ppb),
        grid_spec=grid_spec,
        out_shape=jax.ShapeDtypeStruct((B, 1, n_heads * dh), BF16),
        compiler_params=_params("parallel", "arbitrary"),
        name="moba_decode_attend",
    )(phys.reshape(-1), blk.reshape(-1), _alibi_slopes(n_heads), q, kv_new, kv_new,
      *([cache] * (2 * n_sel * ppb)))


def _nsa_decode_compress_kernel(pt_ref, *refs, n_steps, page, n_groups):
    n_in = len(refs) - 5
    cache_refs = refs[:n_in]
    pe_ref, w1_ref, w2_ref, o_ref, buf_ref = refs[n_in:]
    st = pl.program_id(1)
    pps = n_in // 2
    for i in range(pps):
        row0 = pl.multiple_of((st * pps + i) * page, page)
        for c in range(2 * n_groups):
            buf_ref[c, pl.ds(row0, page), :] = cache_refs[2 * i + c // n_groups][:, c % n_groups, :]

    @pl.when(st == n_steps - 1)
    def _():
        dh = HEAD_DIM
        rows = buf_ref.shape[1] // CMP_STRIDE
        halves = CMP_LEN // CMP_STRIDE
        for c in range(2 * n_groups):
            which = c // n_groups
            parts = [None] * halves
            for l in range(CMP_STRIDE):
                x = buf_ref.at[c][pl.ds(l, rows, stride=CMP_STRIDE), :]
                for hf in range(halves):
                    ll = hf * CMP_STRIDE + l
                    d = _dot(x + pe_ref[which, ll:ll + 1, :], w1_ref[which, ll * dh:(ll + 1) * dh, :])
                    parts[hf] = d if parts[hf] is None else parts[hf] + d
            hid = parts[0]
            for hf in range(1, halves):
                hid = hid + pltpu.roll(parts[hf], rows - hf, axis=0)
            y = _dot(hid * jax.nn.sigmoid(hid), w2_ref[which])
            ridx = lax.broadcasted_iota(jnp.int32, y.shape, 0)
            o_ref[0, c] = jnp.where(ridx < rows - (halves - 1), y, 0.0)


def _nsa_decode_compress(cache, layer, page_table, pe, w1, w2, n_groups):
    B, n_pages = page_table.shape
    page = cache.shape[2]
    dh = HEAD_DIM
    past = n_pages * page
    rows = past // CMP_STRIDE
    hidden = w1.shape[-1]

    pps = DECODE_PAGES_PER_STEP
    assert n_pages % pps == 0
    n_steps = n_pages // pps

    def cache_spec(i, which):
        return pl.BlockSpec((None, None, page, None, n_groups, dh),
                            lambda b, st, pt: (layer, pt[b * n_pages + st * pps + i], 0, which, 0, 0))

    cache_specs = [cache_spec(i, which) for i in range(pps) for which in range(2)]
    grid_spec = pltpu.PrefetchScalarGridSpec(
        num_scalar_prefetch=1,
        grid=(B, n_steps),
        in_specs=cache_specs + [
                  pl.BlockSpec((None, 2, CMP_LEN, dh), lambda b, p, pt: (layer, 0, 0, 0)),
                  pl.BlockSpec((None, 2, CMP_LEN * dh, hidden), lambda b, p, pt: (layer, 0, 0, 0)),
                  pl.BlockSpec((None, 2, hidden, dh), lambda b, p, pt: (layer, 0, 0, 0))],
        out_specs=pl.BlockSpec((1, 2 * n_groups, rows, dh), lambda b, p, pt: (b, 0, 0, 0)),
        scratch_shapes=[pltpu.VMEM((2 * n_groups, past, dh), F32)])
    return pl.pallas_call(
        functools.partial(_nsa_decode_compress_kernel, n_steps=n_steps, page=page, n_groups=n_groups),
        grid_spec=grid_spec,
        out_shape=jax.ShapeDtypeStruct((B, 2 * n_groups, rows, dh), F32),
        compiler_params=_params("parallel", "arbitrary"),
        name="nsa_decode_compress",
    )(page_table.reshape(-1), *([cache] * (2 * pps)), pe, w1, w2)


def _split_heads(q, gs):
    dh = HEAD_DIM
    return jnp.concatenate([q[:, s * dh:(s + 1) * dh] for s in range(gs)], axis=0)


def _group_slopes(slopes_ref, g, gs):
    sub = lax.broadcasted_iota(jnp.int32, (gs, 1), 0)
    slope = jnp.zeros((gs, 1), F32)
    for s in range(gs):
        slope = jnp.where(sub == s, slopes_ref[g * gs + s], slope)
    return slope


def _nsa_select_kernel(slopes_ref, q_ref, kc_ref, vc_ref, ov_ref, ocmp_ref, idx_ref, *, gs, pos):
    g = pl.program_id(1)
    qs = _split_heads(q_ref[0], gs)
    slope = _group_slopes(slopes_ref, g, gs)
    ncp = kc_ref.shape[2]
    c_end = lax.broadcasted_iota(jnp.int32, (1, ncp), 1) * CMP_STRIDE + (CMP_LEN - 1)
    dist_c = pos - c_end
    cmask = dist_c >= 0
    s_c = jnp.where(cmask, _dot_nt(qs, kc_ref[0, 0]) * SCALE - slope * dist_c.astype(F32), NEG)
    e_c = jnp.where(cmask, jnp.exp(s_c - jnp.max(s_c, axis=-1, keepdims=True)), 0.0)
    den = jnp.sum(e_c, axis=-1, keepdims=True)
    p_c = e_c / jnp.where(den > 0.0, den, 1.0)
    ocmp_ref[0, 0] = _dot(p_c, vc_ref[0, 0])
    imp = jnp.sum(_dot(p_c, ov_ref[...]), axis=0, keepdims=True)
    idx = _topk_indices(imp, SEL_TOPK, idx_ref.shape[3])
    idx_ref[0, 0] = jnp.broadcast_to(idx, idx_ref.shape[2:])


def _nsa_decode_select(q, cmp_kv, n_heads, n_groups, pos):
    B = q.shape[0]
    G = n_groups
    gs = n_heads // G
    dh = HEAD_DIM
    R = cmp_kv.shape[2]
    cur = pos // SEL_BLOCK
    assert pos % SEL_BLOCK == 0 and cur >= SEL_TOPK
    c_start = np.arange(R) * CMP_STRIDE
    s_start = np.arange(cur) * SEL_BLOCK
    nc = (pos + 1 - CMP_LEN) // CMP_STRIDE + 1
    overlap = ((c_start[:, None] <= s_start[None, :] + SEL_BLOCK - 1)
               & (c_start[:, None] + CMP_LEN - 1 >= s_start[None, :])
               & (np.arange(R)[:, None] < nc))
    overlap = jnp.asarray(overlap.astype(np.float32), BF16)
    return pl.pallas_call(
        functools.partial(_nsa_select_kernel, gs=gs, pos=pos),
        grid=(B, G),
        in_specs=[pl.BlockSpec(memory_space=pltpu.SMEM),
                  pl.BlockSpec((1, 1, gs * dh), lambda b, g: (b, 0, g)),
                  pl.BlockSpec((1, 1, R, dh), lambda b, g: (b, g, 0, 0)),
                  pl.BlockSpec((1, 1, R, dh), lambda b, g: (b, G + g, 0, 0)),
                  pl.BlockSpec((R, cur), lambda b, g: (0, 0))],
        out_specs=[pl.BlockSpec((1, 1, gs, dh), lambda b, g: (b, g, 0, 0)),
                   pl.BlockSpec((1, 1, 8, 128), lambda b, g: (b, g, 0, 0))],
        out_shape=[jax.ShapeDtypeStruct((B, G, gs, dh), F32), jax.ShapeDtypeStruct((B, G, 8, 128), jnp.int32)],
        compiler_params=_params("parallel", "parallel"),
        name="nsa_decode_select",
    )(_alibi_slopes(n_heads), q, cmp_kv, cmp_kv, overlap)


def _pick_group(ref, g, n_groups):
    out = ref[:, 0, :]
    for gg in range(1, n_groups):
        out = jnp.where(g == gg, ref[:, gg, :], out)
    return out


def _nsa_attend_kernel(pool_blk_ref, blk_ref, slopes_ref, q_ref, gates_ref, ocmp_ref, ksn_ref, vsn_ref,
                       kwn_ref, vwn_ref, kw_ref, vw_ref, *rest, n_groups, gs, pos):
    cache_refs, o_ref = rest[:-1], rest[-1]
    b = pl.program_id(0)
    g = pl.program_id(1)
    dh = HEAD_DIM
    n_sel = len(cache_refs) // 2
    qs = _split_heads(q_ref[0], gs)
    slope = _group_slopes(slopes_ref, g, gs)

    def attend(k_past, v_past, dist, ok, k_new, v_new):
        s = jnp.where(ok, _dot_nt(qs, k_past) * SCALE - slope * dist.astype(F32), NEG)
        s_new = _dot_nt(qs, jnp.broadcast_to(k_new, (gs, dh)))[:, 0:1] * SCALE
        m = jnp.maximum(jnp.max(s, axis=-1, keepdims=True), s_new)
        p = jnp.exp(s - m)
        p_new = jnp.exp(s_new - m)
        den = jnp.sum(p, axis=-1, keepdims=True) + p_new
        acc = _dot(p, v_past) + _mx(p_new).astype(F32) * _mx(v_new).astype(F32)
        return acc / den

    lane = lax.broadcasted_iota(jnp.int32, (1, SEL_BLOCK), 1)
    dists = [pos - (blk_ref[(b * n_groups + g) * n_sel + j] * SEL_BLOCK + lane) for j in range(n_sel)]
    dist = jnp.concatenate(dists, axis=1)
    k_sel = jnp.concatenate([_pick_group(cache_refs[2 * j], g, n_groups) for j in range(n_sel)], axis=0)
    v_sel = jnp.concatenate([_pick_group(cache_refs[2 * j + 1], g, n_groups) for j in range(n_sel)], axis=0)
    o_sel = attend(k_sel, v_sel, dist, dist >= 0, ksn_ref[0], vsn_ref[0])

    n_buf = kw_ref.shape[0]
    dist_w = n_buf - lax.broadcasted_iota(jnp.int32, (1, n_buf), 1)
    o_win = attend(_pick_group(kw_ref, g, n_groups), _pick_group(vw_ref, g, n_groups), dist_w, dist_w < WINDOW,
                   kwn_ref[0], vwn_ref[0])

    gate = jax.nn.sigmoid(gates_ref[0, 0])
    o = gate[:, 0:1] * ocmp_ref[0, 0] + gate[:, 1:2] * o_sel + gate[:, 2:3] * o_win
    for s in range(gs):
        o_ref[0, :, s * dh:(s + 1) * dh] = o[s:s + 1].astype(o_ref.dtype)


def _nsa_decode_attend(q, gates, o_cmp, nsa_new, win_new, cache, win_state, layer, page_table, top_idx,
                       n_heads, n_groups, pos):
    B, n_pages = page_table.shape
    page = cache.shape[2]
    assert cache.shape[3] == 4
    G = n_groups
    gs = n_heads // G
    dh = HEAD_DIM
    per_page = page // SEL_BLOCK
    n_buf = win_state.shape[2]
    assert page % SEL_BLOCK == 0 and pos >= n_buf
    blk = top_idx[:, :, 0, :SEL_TOPK]
    phys = jnp.take_along_axis(page_table[:, None, :], (blk // per_page).reshape(B, 1, -1), axis=2)
    pool_blk = phys.reshape(B, G, SEL_TOPK) * per_page + blk % per_page
    n_sel = SEL_TOPK

    def cache_spec(j, kind):
        def imap(b, g, pool_blk_ref, blk_ref):
            pb = pool_blk_ref[(b * G + g) * n_sel + j]
            return (layer, pb // per_page, pb % per_page, kind, 0, 0)
        return pl.BlockSpec((None, None, SEL_BLOCK, None, G, dh), imap)

    cache_specs = []
    for j in range(n_sel):
        cache_specs += [cache_spec(j, 2), cache_spec(j, 3)]
    tok = lambda col0: pl.BlockSpec((1, 1, dh), lambda b, g, *_: (b, 0, col0 + g))
    win = lambda kv: pl.BlockSpec((None, None, n_buf, None, G, dh), lambda b, g, *_: (layer, b, 0, kv, 0, 0))
    grid_spec = pltpu.PrefetchScalarGridSpec(
        num_scalar_prefetch=2,
        grid=(B, G),
        in_specs=[pl.BlockSpec(memory_space=pltpu.SMEM),
                  pl.BlockSpec((1, 1, gs * dh), lambda b, g, *_: (b, 0, g)),
                  pl.BlockSpec((1, 1, gs, 3), lambda b, g, *_: (b, g, 0, 0)),
                  pl.BlockSpec((1, 1, gs, dh), lambda b, g, *_: (b, g, 0, 0)),
                  tok(2 * G), tok(3 * G), tok(0), tok(G), win(0), win(1)] + cache_specs,
        out_specs=pl.BlockSpec((1, 1, gs * dh), lambda b, g, *_: (b, 0, g)))
    return pl.pallas_call(
        functools.partial(_nsa_attend_kernel, n_groups=G, gs=gs, pos=pos),
        grid_spec=grid_spec,
        out_shape=jax.ShapeDtypeStruct((B, 1, n_heads * dh), BF16),
        compiler_params=_params("parallel", "arbitrary"),
        name="nsa_decode_attend",
    )(pool_blk.reshape(-1), blk.reshape(-1), _alibi_slopes(n_heads), q, gates, o_cmp,
      nsa_new, nsa_new, win_new, win_new, win_state, win_state, *([cache] * (2 * n_sel)))


def _trunk(x3, start, past, ln_g, ln_b, ffn_gate, ffn_up, ffn_down, attn_w_in, attn_w_out,
           cmp_pe, cmp_w1, cmp_w2, pool_w, pool_scale):
    B, T, D = x3.shape
    M = B * T
    depth = ffn_gate.shape[0]
    d_ff = ffn_gate.shape[-1]
    alpha = (2.0 * depth) ** 0.25
    decode = past is not None
    dh = HEAD_DIM
    n_heads = D // dh
    H_A = n_heads // 2
    NH = n_heads - H_A
    G = cmp_pe.shape[1]
    gs = NH // G
    moba_w, nsa_w, kvw = H_A * dh, NH * dh, G * dh
    tm = min(M, FFN_ROW_TILE)
    tm_ln = min(M, 256)
    pool_w_mx = pool_w.astype(MXU_DTYPE)
    if decode:
        assert T == 1
        cache_moba_kv, cache_nsa_kv, state_nsa_win, state_pool, page_table = past

    x = x3.reshape(M, D)
    xb = x.astype(BF16)
    new_moba, new_nsa, new_win, new_pool = [], [], [], []

    def ffn(x, xb, layer, sub, ln_idx):
        hidden = yield (xb, layer, sub)
        y = _matmul(hidden, ffn_down, (layer, sub), 0, D, F32, tm=tm, tn=256, kchunk=d_ff // 2, name="ffn_down")
        return _res_ln(y, x, ln_g[layer, ln_idx], ln_b[layer, ln_idx], alpha=alpha, coef=0.5, tm=tm_ln)

    for layer in range(depth):
        x, xb = yield from ffn(x, xb, layer, 0, 0)
        if layer % 2 == 0:
            a = layer // 2
            proj = functools.partial(_matmul, xb, attn_w_in, (a,), tm=tm, name="in_proj")
            q_a = proj(0, moba_w, BF16, tn=256)
            moba_kv = proj(moba_w, 2 * moba_w, F32, tn=256)
            q_b = proj(3 * moba_w, nsa_w, BF16, tn=256)
            c0 = 3 * moba_w + nsa_w
            nsa_kv = proj(c0, 4 * kvw, F32, tn=256)
            win_kv = proj(c0 + 4 * kvw, 2 * kvw, F32, tn=256)
            gates = proj(c0 + 6 * kvw, 3 * NH, F32, tn=128)
            moba_new = moba_kv.reshape(B, T, 2, H_A, dh)
            nsa_new = nsa_kv.reshape(B, T, 4, G, dh)
            win_new = win_kv.reshape(B, T, 2, G, dh)
            q_a3, q_b3 = q_a.reshape(B, T, moba_w), q_b.reshape(B, T, nsa_w)
            moba_kv3, nsa_kv3, win_kv3 = (moba_kv.reshape(B, T, 2 * moba_w), nsa_kv.reshape(B, T, 4 * kvw),
                                          win_kv.reshape(B, T, 2 * kvw))
            if not decode:
                o_a = _moba_prompt(q_a3, moba_kv3, H_A)
                kv16 = nsa_new[:, :, 0:2].transpose(0, 2, 3, 1, 4).reshape(B, 2, G, T // 16, 16 * dh)
                cmp_kv = _compress_prompt(kv16, cmp_pe[a].reshape(2, 2, 16 * dh), cmp_w1, cmp_w2, a)
                gates_g = gates.reshape(B, T, 3, G, gs).transpose(0, 3, 1, 2, 4).reshape(B, G, T, 3 * gs)
                o_b = _nsa_prompt(q_b3, gates_g, cmp_kv, nsa_kv3, win_kv3, NH, G)
                win_state = win_new[:, -min(WINDOW, T):]
            else:
                top_a = _moba_decode_select(q_a3, cache_moba_kv, a, page_table, H_A)
                o_a = _moba_decode_attend(q_a3, moba_kv3, cache_moba_kv, a, page_table, top_a, H_A, start)
                cmp_kv = _nsa_decode_compress(cache_nsa_kv, a, page_table, cmp_pe, cmp_w1, cmp_w2, G)
                o_cmp, top_b = _nsa_decode_select(q_b3, cmp_kv, NH, G, start)
                gates_g = gates.reshape(B, 3, G, gs).transpose(0, 2, 3, 1)
                o_b = _nsa_decode_attend(q_b3, gates_g, o_cmp, nsa_kv3, win_kv3, cache_nsa_kv, state_nsa_win, a,
                                         page_table, top_b, NH, G, start)
                win_state = jnp.concatenate([state_nsa_win[a][:, T:], win_new], axis=1)
            mixed = jnp.concatenate([o_a, o_b], axis=-1).reshape(M, D)
            new_moba.append(moba_new)
            new_nsa.append(nsa_new)
            new_win.append(win_state)
            y = _matmul(mixed, attn_w_out, (a,), 0, D, F32, tm=tm, tn=256, name="out_proj")
            x, xb = _res_ln(y, x, ln_g[layer, 1], ln_b[layer, 1], alpha=alpha, coef=1.0, tm=tm_ln)
        else:
            p = layer // 2
            x3c = x.reshape(B, T, D)
            if decode:
                prev = jnp.concatenate([state_pool[p], x3c], axis=1)
                new_pool.append(prev[:, -POOL_BUF:])
                tt = T
            else:
                prev = x3c
                new_pool.append(x3c[:, -POOL_BUF:])
                tt = 256
            xo, xbo = _pool_mixer_ln(x3c, prev, pool_w_mx[p], pool_scale[p], ln_g[layer, 1], ln_b[layer, 1],
                                     alpha=alpha, start=start, tt=tt, decode=decode)
            x, xb = xo.reshape(M, D), xbo.reshape(M, D)
        x, xb = yield from ffn(x, xb, layer, 1, 2)
    return (x.reshape(B, T, D), jnp.stack(new_moba), jnp.stack(new_nsa), jnp.stack(new_win),
            jnp.stack(new_pool))


def kernel(x_prompt, x_sample, cache_moba_kv, cache_nsa_kv, state_nsa_win, state_pool, page_table,
           ln_g, ln_b, ffn_gate, ffn_up, ffn_down, attn_w_in, attn_w_out, cmp_pe, cmp_w1, cmp_w2,
           pool_w, pool_scale):
    weights = (ln_g, ln_b, ffn_gate, ffn_up, ffn_down, attn_w_in, attn_w_out, cmp_pe, cmp_w1, cmp_w2,
               pool_w, pool_scale)
    past_len = page_table.shape[1] * cache_moba_kv.shape[2]
    prompt = _trunk(x_prompt, 0, None, *weights)
    sample = _trunk(x_sample, past_len, (cache_moba_kv, cache_nsa_kv, state_nsa_win, state_pool, page_table),
                    *weights)
    req_p, req_s = next(prompt), next(sample)
    tm = min(x_prompt.shape[0] * x_prompt.shape[1], FFN_ROW_TILE)
    while True:
        (xb_p, layer, sub), (xb_s, layer_s, sub_s) = req_p, req_s
        assert (layer, sub) == (layer_s, sub_s)
        h_p, h_s = _ffn_up(xb_p, xb_s, ffn_gate, ffn_up, layer, sub, tm)
        try:
            req_p = prompt.send(h_p)
            req_s = sample.send(h_s)
        except StopIteration as done_p:
            try:
                sample.send(h_s)
            except StopIteration as done_s:
                y_p, moba_p, nsa_p, win_p, pool_p = done_p.value
                y_s, moba_s, nsa_s, win_s, pool_s = done_s.value
                return (y_p, y_s, moba_p, nsa_p, win_p, pool_p, moba_s, nsa_s, win_s, pool_s)
            raise AssertionError("token groups finished at different FFNs")
```
